```python
import math
import jax, jax.numpy as jnp
from jax import lax
import numpy as np

D_MODEL = 1024
BATCH = 1
SEQ = 16384
DEPTH = 4

N_META = 16
N_HEADS = 16
N_KV_HEADS = 4
HEAD_DIM = 64
Q_GROUP = N_HEADS // N_KV_HEADS
WINDOW = 128
BLOCK = 128
SSM_WIDTH = D_MODEL // 2
SSM_GROUP_CH = 16
SSM_GROUPS = SSM_WIDTH // SSM_GROUP_CH
SSM_STATE = 64
N_DIR = 2
DT_MIN = 1e-3
DT_MAX = 1e-1
D_FF = 2816
EPS = 1e-6
NEG = -1e30

ATTN_Q = N_HEADS * HEAD_DIM
ATTN_KV = N_KV_HEADS * HEAD_DIM
SPLITS = [ATTN_Q, ATTN_Q + ATTN_KV, ATTN_Q + 2 * ATTN_KV,
          ATTN_Q + 2 * ATTN_KV + SSM_WIDTH,
          ATTN_Q + 2 * ATTN_KV + SSM_WIDTH + D_MODEL]
IN_COLS = ATTN_Q + 2 * ATTN_KV + SSM_WIDTH + 2 * D_MODEL

kernel_name = "hybrid_s5_swa_macaron_encoder"


def alibi_slopes():
    s = 2.0 ** (-8.0 * np.arange(1, N_HEADS + 1) / N_HEADS)
    return jnp.asarray(s.reshape(N_KV_HEADS, Q_GROUP), dtype=jnp.float32)


def rmsnorm(x, g):
    xf = x.astype(jnp.float32)
    y = xf * lax.rsqrt(jnp.mean(xf * xf, axis=-1, keepdims=True) + EPS)
    return (y * g.astype(jnp.float32)).astype(x.dtype)


def swiglu(h, w_gate, w_up, w_down):
    return (jax.nn.silu(h @ w_gate) * (h @ w_up)) @ w_down


def s5_direction(ug, lam_re, lam_im, log_dt, b_re, b_im, c_re, c_im, reverse):
    f32 = jnp.float32
    lam = lax.complex(lam_re.astype(f32), lam_im.astype(f32))
    dt = jnp.exp(log_dt.astype(f32))[:, None]
    lam_bar = jnp.exp(lam * dt)
    b_bar = ((lam_bar - 1.0) / lam)[:, :, None] * lax.complex(b_re.astype(f32), b_im.astype(f32))
    c_mat = lax.complex(c_re.astype(f32), c_im.astype(f32))
    bu = jnp.einsum('blgc,gpc->blgp', ug, b_bar)
    a = jnp.broadcast_to(lam_bar, bu.shape)

    def combine(e1, e2):
        a1, b1 = e1
        a2, b2 = e2
        return a1 * a2, a2 * b1 + b2

    _, states = lax.associative_scan(combine, (a, bu), reverse=reverse, axis=1)
    return jnp.einsum('blgp,gcp->blgc', states, c_mat).real


def ssm_branch(u, lam_re, lam_im, log_dt, b_re, b_im, c_re, c_im, d, w_glu):
    B, L, _ = u.shape
    uf = u.astype(jnp.float32)
    ug = uf.reshape(B, L, SSM_GROUPS, SSM_GROUP_CH)
    y = jnp.zeros_like(ug)
    for dr in range(N_DIR):
        y = y + s5_direction(ug, lam_re[dr], lam_im[dr], log_dt[dr], b_re[dr], b_im[dr],
                             c_re[dr], c_im[dr], reverse=(dr == 1))
    y = y.reshape(B, L, SSM_WIDTH) + d.astype(jnp.float32) * uf
    z = jax.nn.gelu(y).astype(u.dtype)
    return z * jax.nn.sigmoid(z @ w_glu)


def windowed_gqa(q, k, v, sink):
    B, L = q.shape[:2]
    pad = BLOCK - N_META
    Lp = L + pad
    nb = Lp // BLOCK
    k_meta, v_meta = k[:, :N_META], v[:, :N_META]

    def to_blocks(t):
        tp = jnp.pad(t, ((0, 0), (pad, 0), (0, 0), (0, 0)))
        return tp.reshape(B, nb, BLOCK, *t.shape[2:])

    def band(t):
        tb = jnp.pad(to_blocks(t), ((0, 0), (1, 1), (0, 0), (0, 0), (0, 0)))
        return jnp.concatenate([tb[:, :-2], tb[:, 1:-1], tb[:, 2:]], axis=2)

    qb = to_blocks(q).reshape(B, nb, BLOCK, N_KV_HEADS, Q_GROUP, HEAD_DIM)
    kband, vband = band(k), band(v)
    scale = HEAD_DIM ** -0.5

    qi = jnp.arange(BLOCK)
    sj = jnp.arange(3 * BLOCK)
    dist = jnp.abs(qi[:, None] + BLOCK - sj[None, :])
    kpos = (jnp.arange(nb)[:, None] - 1) * BLOCK + sj[None, :]
    kvalid = (kpos >= BLOCK) & (kpos < Lp)
    valid = (dist <= WINDOW)[None] & kvalid[:, None, :]
    slopes = alibi_slopes()

    s_band = jnp.einsum('bnqkgd,bnskd->bnkgqs', qb, kband,
                        preferred_element_type=jnp.float32) * scale
    s_band = s_band - slopes[:, :, None, None] * dist.astype(jnp.float32)
    s_band = jnp.where(valid[None, :, None, None], s_band, NEG)
    s_meta = jnp.einsum('bnqkgd,bmkd->bnkgqm', qb, k_meta,
                        preferred_element_type=jnp.float32) * scale
    s_sink = jnp.broadcast_to(sink.astype(jnp.float32).reshape(N_KV_HEADS, Q_GROUP)[None, None, :, :, None, None],
                              (B, nb, N_KV_HEADS, Q_GROUP, BLOCK, 1))
    p = jax.nn.softmax(jnp.concatenate([s_band, s_meta, s_sink], axis=-1), axis=-1)
    p_band = p[..., :3 * BLOCK].astype(v.dtype)
    p_meta = p[..., 3 * BLOCK:3 * BLOCK + N_META].astype(v.dtype)
    out = (jnp.einsum('bnkgqs,bnskd->bnqkgd', p_band, vband)
           + jnp.einsum('bnkgqm,bmkd->bnqkgd', p_meta, v_meta))
    return out.reshape(B, Lp, N_HEADS * HEAD_DIM)[:, pad:]


def mixer(h, w_in, lam_re, lam_im, log_dt, b_re, b_im, c_re, c_im, d, w_glu, sink,
          w_branch_ssm, w_branch_attn, w_out):
    B, L, _ = h.shape
    proj = h @ w_in
    q, k, v, u, g_s, g_a = jnp.split(proj, SPLITS, axis=-1)
    y_attn = windowed_gqa(q.reshape(B, L, N_HEADS, HEAD_DIM),
                          k.reshape(B, L, N_KV_HEADS, HEAD_DIM),
                          v.reshape(B, L, N_KV_HEADS, HEAD_DIM), sink)
    y_ssm = ssm_branch(u, lam_re, lam_im, log_dt, b_re, b_im, c_re, c_im, d, w_glu)
    merged = (jax.nn.sigmoid(g_s) * (y_ssm @ w_branch_ssm)
              + jax.nn.sigmoid(g_a) * (y_attn @ w_branch_attn))
    return merged @ w_out


def setup_inputs(seed: int = 0) -> dict:
    key = jax.random.key(seed)
    ks = jax.random.split(key, 32)
    f32 = jnp.float32

    def nrm(k, shape, fan_in):
        return jax.random.normal(k, shape, f32) * fan_in ** -0.5

    def gain(k, shape):
        return 1.0 + 0.05 * jax.random.normal(k, shape, f32)

    G, P, CH = SSM_GROUPS, SSM_STATE, SSM_GROUP_CH
    n = jnp.arange(P, dtype=f32)
    return {
        "x": jax.random.normal(ks[0], (BATCH, SEQ, D_MODEL), f32),
        "meta_tokens": jax.random.normal(ks[1], (N_META, D_MODEL), f32),
        "ffn1_norm": gain(ks[2], (DEPTH, D_MODEL)),
        "ffn1_w_gate": nrm(ks[3], (DEPTH, D_MODEL, D_FF), D_MODEL),
        "ffn1_w_up": nrm(ks[4], (DEPTH, D_MODEL, D_FF), D_MODEL),
        "ffn1_w_down": nrm(ks[5], (DEPTH, D_FF, D_MODEL), D_FF),
        "mix_norm": gain(ks[6], (DEPTH, D_MODEL)),
        "w_in": nrm(ks[7], (DEPTH, D_MODEL, IN_COLS), D_MODEL),
        "ssm_lam_re": -0.5 + 0.01 * jax.random.normal(ks[8], (DEPTH, N_DIR, G, P), f32),
        "ssm_lam_im": math.pi * n + 0.01 * jax.random.normal(ks[9], (DEPTH, N_DIR, G, P), f32),
        "ssm_log_dt": jax.random.uniform(ks[10], (DEPTH, N_DIR, G), f32,
                                         minval=math.log(DT_MIN), maxval=math.log(DT_MAX)),
        "ssm_b_re": nrm(ks[11], (DEPTH, N_DIR, G, P, CH), 2 * CH),
        "ssm_b_im": nrm(ks[12], (DEPTH, N_DIR, G, P, CH), 2 * CH),
        "ssm_c_re": nrm(ks[13], (DEPTH, N_DIR, G, CH, P), 2 * P),
        "ssm_c_im": nrm(ks[14], (DEPTH, N_DIR, G, CH, P), 2 * P),
        "ssm_d": jax.random.normal(ks[15], (DEPTH, SSM_WIDTH), f32),
        "ssm_w_glu": nrm(ks[16], (DEPTH, SSM_WIDTH, SSM_WIDTH), SSM_WIDTH),
        "attn_sink": 0.5 * jax.random.normal(ks[17], (DEPTH, N_HEADS), f32),
        "w_branch_ssm": nrm(ks[18], (DEPTH, SSM_WIDTH, D_MODEL), SSM_WIDTH),
        "w_branch_attn": nrm(ks[19], (DEPTH, ATTN_Q, D_MODEL), ATTN_Q),
        "w_out": nrm(ks[20], (DEPTH, D_MODEL, D_MODEL), D_MODEL),
        "ffn2_norm": gain(ks[21], (DEPTH, D_MODEL)),
        "ffn2_w_gate": nrm(ks[22], (DEPTH, D_MODEL, D_FF), D_MODEL),
        "ffn2_w_up": nrm(ks[23], (DEPTH, D_MODEL, D_FF), D_MODEL),
        "ffn2_w_down": nrm(ks[24], (DEPTH, D_FF, D_MODEL), D_FF),
        "final_norm": gain(ks[25], (D_MODEL,)),
    }


def reference(x, meta_tokens, ffn1_norm, ffn1_w_gate, ffn1_w_up, ffn1_w_down,
              mix_norm, w_in, ssm_lam_re, ssm_lam_im, ssm_log_dt, ssm_b_re, ssm_b_im,
              ssm_c_re, ssm_c_im, ssm_d, ssm_w_glu, attn_sink, w_branch_ssm,
              w_branch_attn, w_out, ffn2_norm, ffn2_w_gate, ffn2_w_up, ffn2_w_down,
              final_norm):
    B = x.shape[0]
    meta = jnp.broadcast_to(meta_tokens.astype(x.dtype)[None], (B, N_META, D_MODEL))
    h = jnp.concatenate([meta, x], axis=1)
    for l in range(DEPTH):
        h = h + 0.5 * swiglu(rmsnorm(h, ffn1_norm[l]), ffn1_w_gate[l], ffn1_w_up[l], ffn1_w_down[l])
        h = h + mixer(rmsnorm(h, mix_norm[l]), w_in[l], ssm_lam_re[l], ssm_lam_im[l],
                      ssm_log_dt[l], ssm_b_re[l], ssm_b_im[l], ssm_c_re[l], ssm_c_im[l],
                      ssm_d[l], ssm_w_glu[l], attn_sink[l], w_branch_ssm[l],
                      w_branch_attn[l], w_out[l])
        h = h + 0.5 * swiglu(rmsnorm(h, ffn2_norm[l]), ffn2_w_gate[l], ffn2_w_up[l], ffn2_w_down[l])
    h = rmsnorm(h, final_norm)
    return h[:, N_META:]
```

```python
import functools
import math

import numpy as np
import jax
import jax.numpy as jnp
from jax import lax
from jax.experimental import pallas as pl
from jax.experimental.pallas import tpu as pltpu

f32 = jnp.float32
bf16 = jnp.bfloat16

D_MODEL = 1024
SEQ = 16384
DEPTH = 4
N_META = 16
N_HEADS = 16
N_KV_HEADS = 4
HEAD_DIM = 64
Q_GROUP = N_HEADS // N_KV_HEADS
WINDOW = 128
BLOCK = 128
SSM_WIDTH = 512
SSM_GROUP_CH = 16
SSM_GROUPS = 32
SSM_STATE = 64
D_FF = 2816
EPS = 1e-6
NEG = -1e30

ATTN_Q = N_HEADS * HEAD_DIM
ATTN_KV = N_KV_HEADS * HEAD_DIM
IN_COLS = 4096
COL_K = ATTN_Q
COL_V = ATTN_Q + ATTN_KV
COL_U = ATTN_Q + 2 * ATTN_KV
COL_GS = COL_U + SSM_WIDTH
COL_GA = COL_GS + D_MODEL

PAD = BLOCK - N_META
LP = PAD + N_META + SEQ
NB = LP // BLOCK

LANES = 128
SUBLANES = 8
VMEM_LIMIT = 56 * 1024 * 1024

TM = 688
TF = D_FF // 2
TN_IN = 2048

SEG = 172
T_SSM = SUBLANES * SEG
NT_SSM = LP // T_SSM
N_SLAB = SSM_WIDTH // LANES
G_SLAB = SSM_GROUPS // N_SLAB
SLAB_STATE = G_SLAB * SSM_STATE


def _cparams(sem):
    return pltpu.CompilerParams(dimension_semantics=sem, vmem_limit_bytes=VMEM_LIMIT)


def _rmsnorm(x, g):
    ms = jnp.mean(x * x, axis=-1, keepdims=True)
    return x * lax.rsqrt(ms + EPS) * g


def _ffn_kernel(x_ref, g_ref, wg_ref, wu_ref, wd_ref, o_ref, xn_ref, acc_ref):
    j = pl.program_id(1)

    @pl.when(j == 0)
    def _():
        xn_ref[...] = _rmsnorm(x_ref[...], g_ref[...]).astype(bf16)
        acc_ref[...] = jnp.zeros_like(acc_ref)

    xn = xn_ref[...]
    a = jnp.dot(xn, wg_ref[...], preferred_element_type=f32)
    b = jnp.dot(xn, wu_ref[...], preferred_element_type=f32)
    mid = (a * jax.nn.sigmoid(a) * b).astype(bf16)
    acc_ref[...] += jnp.dot(mid, wd_ref[...], preferred_element_type=f32)

    @pl.when(j == pl.num_programs(1) - 1)
    def _():
        o_ref[...] = x_ref[...] + 0.5 * acc_ref[...]


def _ffn(h, g, wg, wu, wd):
    return pl.pallas_call(
        _ffn_kernel,
        grid=(LP // TM, D_FF // TF),
        in_specs=[
            pl.BlockSpec((TM, D_MODEL), lambda i, j: (i, 0)),
            pl.BlockSpec((1, D_MODEL), lambda i, j: (0, 0)),
            pl.BlockSpec((D_MODEL, TF), lambda i, j: (0, j)),
            pl.BlockSpec((D_MODEL, TF), lambda i, j: (0, j)),
            pl.BlockSpec((TF, D_MODEL), lambda i, j: (j, 0)),
        ],
        out_specs=pl.BlockSpec((TM, D_MODEL), lambda i, j: (i, 0)),
        out_shape=jax.ShapeDtypeStruct((LP, D_MODEL), f32),
        scratch_shapes=[pltpu.VMEM((TM, D_MODEL), bf16), pltpu.VMEM((TM, D_MODEL), f32)],
        compiler_params=_cparams(("parallel", "arbitrary")),
        name="ffn",
    )(h, g, wg, wu, wd)


def _inproj_kernel(x_ref, g_ref, w_ref, o_ref, xn_ref):
    @pl.when(pl.program_id(1) == 0)
    def _():
        xn_ref[...] = _rmsnorm(x_ref[...], g_ref[...]).astype(bf16)

    o_ref[...] = jnp.dot(xn_ref[...], w_ref[...], preferred_element_type=f32).astype(bf16)


def _inproj(h, g, w):
    return pl.pallas_call(
        _inproj_kernel,
        grid=(LP // TM, IN_COLS // TN_IN),
        in_specs=[
            pl.BlockSpec((TM, D_MODEL), lambda i, j: (i, 0)),
            pl.BlockSpec((1, D_MODEL), lambda i, j: (0, 0)),
            pl.BlockSpec((D_MODEL, TN_IN), lambda i, j: (0, j)),
        ],
        out_specs=pl.BlockSpec((TM, TN_IN), lambda i, j: (i, j)),
        out_shape=jax.ShapeDtypeStruct((LP, IN_COLS), bf16),
        scratch_shapes=[pltpu.VMEM((TM, D_MODEL), bf16)],
        compiler_params=_cparams(("parallel", "arbitrary")),
        name="inproj",
    )(h, g, w)


def _attn_kernel(sink_ref, q_ref, kp_ref, kc_ref, kn_ref, km_ref,
                 vp_ref, vc_ref, vn_ref, vm_ref, bias_ref, o_ref):
    outs = []
    for kh in range(N_KV_HEADS):
        sl = slice(kh * HEAD_DIM, (kh + 1) * HEAD_DIM)
        k_all = jnp.concatenate([kp_ref[:, sl], kc_ref[:, sl], kn_ref[:, sl], km_ref[:, sl]], axis=0)
        v_all = jnp.concatenate([vp_ref[:, sl], vc_ref[:, sl], vn_ref[:, sl], vm_ref[:, sl]], axis=0)
        for g in range(Q_GROUP):
            h = kh * Q_GROUP + g
            qh = q_ref[:, h * HEAD_DIM:(h + 1) * HEAD_DIM]
            s = lax.dot_general(qh, k_all, (((1,), (1,)), ((), ())),
                                preferred_element_type=f32) + bias_ref[0, h]
            sink = sink_ref[h]
            m = jnp.maximum(jnp.max(s, axis=-1, keepdims=True), sink)
            p = jnp.exp(s - m)
            denom = jnp.sum(p, axis=-1, keepdims=True) + jnp.exp(sink - m)
            o = jnp.dot(p.astype(bf16), v_all, preferred_element_type=f32)
            outs.append(o / denom)
    o_ref[...] = jnp.concatenate(outs, axis=1).astype(bf16)


def _bias_variant(n):
    return jnp.where(n == 0, 0, jnp.where(n == 1, 1, jnp.where(n == NB - 1, 3, 2)))


def _attn(proj, sink, bias):
    kv = lambda col, fn: pl.BlockSpec((BLOCK, ATTN_KV), lambda n: (fn(n), col))
    prev = lambda n: jnp.maximum(n - 1, 0)
    cur = lambda n: n
    nxt = lambda n: jnp.minimum(n + 1, NB - 1)
    first = lambda n: 0
    ck, cv = COL_K // ATTN_KV, COL_V // ATTN_KV
    return pl.pallas_call(
        _attn_kernel,
        grid=(NB,),
        in_specs=[
            pl.BlockSpec(memory_space=pltpu.SMEM),
            pl.BlockSpec((BLOCK, ATTN_Q), lambda n: (n, 0)),
            kv(ck, prev), kv(ck, cur), kv(ck, nxt), kv(ck, first),
            kv(cv, prev), kv(cv, cur), kv(cv, nxt), kv(cv, first),
            pl.BlockSpec((1, N_HEADS, BLOCK, 4 * BLOCK), lambda n: (_bias_variant(n), 0, 0, 0)),
        ],
        out_specs=pl.BlockSpec((BLOCK, ATTN_Q), lambda n: (n, 0)),
        out_shape=jax.ShapeDtypeStruct((LP, ATTN_Q), bf16),
        compiler_params=_cparams(("parallel",)),
        name="attn",
    )(sink, proj, proj, proj, proj, proj, proj, proj, proj, proj, bias)


def _attn_bias():
    qi = np.arange(BLOCK)[:, None]
    sj = np.arange(3 * BLOCK)[None, :]
    dist = np.abs(qi + BLOCK - sj)
    slopes = 2.0 ** (-8.0 * np.arange(1, N_HEADS + 1) / N_HEADS)
    band = -slopes[:, None, None] * dist[None].astype(np.float64)
    in_win = (dist <= WINDOW)[None]
    meta = np.where(np.arange(BLOCK) >= PAD, 0.0, NEG)[None, None, :]
    meta = np.broadcast_to(meta, (N_HEADS, BLOCK, BLOCK))
    out = []
    for blk_ok in ((False, False, True), (False, True, True), (True, True, True), (True, True, False)):
        kvalid = np.repeat(np.asarray(blk_ok), BLOCK)[None, None, :]
        b = np.where(in_win & kvalid, band, NEG)
        out.append(np.concatenate([b, meta], axis=-1))
    return jnp.asarray(np.stack(out), dtype=f32)


def _cmul(ar, ai, br, bi):
    return ar * br - ai * bi, ar * bi + ai * br


def _ssm_kernel(u_ref, bm_ref, cm_ref, a_ref, as_ref, y_ref, bu_ref, xs_ref, carry_ref):
    i = pl.program_id(0)

    @pl.when(i == 0)
    def _():
        carry_ref[...] = jnp.zeros_like(carry_ref)

    row = lax.broadcasted_iota(jnp.int32, (SUBLANES, SLAB_STATE), 0)

    def slab_body(ds, _):
        d = ds // N_SLAB
        s = ds % N_SLAB
        bu_ref[...] = jnp.dot(u_ref[d, 0, s], bm_ref[d, s], preferred_element_type=f32)
        a = a_ref[d, s]
        are = jnp.broadcast_to(a[0], (SUBLANES, SLAB_STATE))
        aim = jnp.broadcast_to(a[1], (SUBLANES, SLAB_STATE))

        def step(j, x):
            r0 = pl.multiple_of(j * SUBLANES, SUBLANES)
            blk = bu_ref[pl.ds(r0, SUBLANES), :]
            pr, pi = _cmul(are, aim, x[0], x[1])
            return pr + blk[:, :SLAB_STATE], pi + blk[:, SLAB_STATE:]

        zero = jnp.zeros((SUBLANES, SLAB_STATE), f32)
        er, ei = lax.fori_loop(0, SEG, step, (zero, zero))

        asg = as_ref[d, s]
        asr = jnp.broadcast_to(asg[0], (SUBLANES, SLAB_STATE))
        asi = jnp.broadcast_to(asg[1], (SUBLANES, SLAB_STATE))
        cr = carry_ref[ds, 0]
        ci = carry_ref[ds, 1]
        cin_r, cin_i = cr, ci
        for r in range(1, SUBLANES + 1):
            pr, pi = _cmul(asr, asi, cr, ci)
            cr = pr + jnp.broadcast_to(er[r - 1:r], (SUBLANES, SLAB_STATE))
            ci = pi + jnp.broadcast_to(ei[r - 1:r], (SUBLANES, SLAB_STATE))
            if r < SUBLANES:
                cin_r = jnp.where(row >= r, cr, cin_r)
                cin_i = jnp.where(row >= r, ci, cin_i)
        carry_ref[ds, 0] = cr
        carry_ref[ds, 1] = ci

        def step2(j, x):
            xr, xi = step(j, x)
            r0 = pl.multiple_of(j * SUBLANES, SUBLANES)
            xs_ref[pl.ds(r0, SUBLANES), :SLAB_STATE] = xr
            xs_ref[pl.ds(r0, SUBLANES), SLAB_STATE:] = xi
            return xr, xi

        lax.fori_loop(0, SEG, step2, (cin_r, cin_i))
        y_ref[d, 0, s] = jnp.dot(xs_ref[...].astype(bf16), cm_ref[d, s], preferred_element_type=f32)
        return 0

    lax.fori_loop(0, 2 * N_SLAB, slab_body, 0)


def _ssm(u2, bm, cm, a, asg):
    return pl.pallas_call(
        _ssm_kernel,
        grid=(NT_SSM,),
        in_specs=[
            pl.BlockSpec((2, 1, N_SLAB, T_SSM, LANES), lambda i: (0, i, 0, 0, 0)),
            pl.BlockSpec((2, N_SLAB, LANES, 2 * SLAB_STATE), lambda i: (0, 0, 0, 0)),
            pl.BlockSpec((2, N_SLAB, 2 * SLAB_STATE, LANES), lambda i: (0, 0, 0, 0)),
            pl.BlockSpec((2, N_SLAB, 2, 1, SLAB_STATE), lambda i: (0, 0, 0, 0, 0)),
            pl.BlockSpec((2, N_SLAB, 2, 1, SLAB_STATE), lambda i: (0, 0, 0, 0, 0)),
        ],
        out_specs=pl.BlockSpec((2, 1, N_SLAB, T_SSM, LANES), lambda i: (0, i, 0, 0, 0)),
        out_shape=jax.ShapeDtypeStruct((2, NT_SSM, N_SLAB, T_SSM, LANES), f32),
        scratch_shapes=[
            pltpu.VMEM((T_SSM, 2 * SLAB_STATE), f32),
            pltpu.VMEM((T_SSM, 2 * SLAB_STATE), f32),
            pltpu.VMEM((2 * N_SLAB, 2, SUBLANES, SLAB_STATE), f32),
        ],
        compiler_params=_cparams(("arbitrary",)),
        name="ssm",
    )(u2, bm, cm, a, asg)


def _ssm_permute(u):
    a = u.reshape(NT_SSM, SUBLANES, SEG, N_SLAB, LANES)
    return a.transpose(0, 3, 2, 1, 4).reshape(NT_SSM, N_SLAB, T_SSM, LANES)


def _ssm_unpermute(y):
    a = y.reshape(NT_SSM, N_SLAB, SEG, SUBLANES, LANES)
    return a.transpose(0, 3, 2, 1, 4).reshape(LP, SSM_WIDTH)


def _ssm_params(lam_re, lam_im, log_dt, b_re, b_im, c_re, c_im):
    dt = jnp.exp(log_dt)[..., None]
    mag = jnp.exp(lam_re * dt)
    ar, ai = mag * jnp.cos(lam_im * dt), mag * jnp.sin(lam_im * dt)
    mag_s = jnp.exp(lam_re * dt * SEG)
    sr, si = mag_s * jnp.cos(lam_im * dt * SEG), mag_s * jnp.sin(lam_im * dt * SEG)
    den = lam_re * lam_re + lam_im * lam_im
    fr = ((ar - 1.0) * lam_re + ai * lam_im) / den
    fi = (ai * lam_re - (ar - 1.0) * lam_im) / den
    bbr = fr[..., None] * b_re - fi[..., None] * b_im
    bbi = fr[..., None] * b_im + fi[..., None] * b_re
    eye = jnp.eye(G_SLAB, dtype=f32)

    def bmat(b):
        b = b.reshape(DEPTH, 2, N_SLAB, G_SLAB, SSM_STATE, SSM_GROUP_CH)
        m = jnp.einsum('ldsgpi,gh->ldsgihp', b, eye)
        return m.reshape(DEPTH, 2, N_SLAB, LANES, SLAB_STATE)

    def cmat(c):
        c = c.reshape(DEPTH, 2, N_SLAB, G_SLAB, SSM_GROUP_CH, SSM_STATE)
        m = jnp.einsum('ldsgop,gh->ldshpgo', c, eye)
        return m.reshape(DEPTH, 2, N_SLAB, SLAB_STATE, LANES)

    bm = jnp.concatenate([bmat(bbr), bmat(bbi)], axis=-1).astype(bf16)
    cm = jnp.concatenate([cmat(c_re), -cmat(c_im)], axis=-2).astype(bf16)

    def vec(r, i):
        v = jnp.stack([r, i], axis=2)
        v = v.reshape(DEPTH, 2, 2, N_SLAB, 1, SLAB_STATE)
        return v.transpose(0, 1, 3, 2, 4, 5)

    return bm, cm, vec(ar, ai), vec(sr, si)


def _gelu_tanh(x):
    c = math.sqrt(2.0 / math.pi)
    return 0.5 * x * (1.0 + jnp.tanh(c * (x + 0.044715 * (x * x * x))))


def _merge_kernel(h_ref, yf_ref, yb_ref, u_ref, at_ref, gs_ref, ga_ref, d_ref,
                  wglu_ref, wbs_ref, wba_ref, wout_ref, o_ref):
    y = yf_ref[...] + yb_ref[...] + d_ref[...] * u_ref[...].astype(f32)
    z = _gelu_tanh(y)
    gl = jnp.dot(z.astype(bf16), wglu_ref[...], preferred_element_type=f32)
    ys = (z * jax.nn.sigmoid(gl)).astype(bf16)
    m1 = jnp.dot(ys, wbs_ref[...], preferred_element_type=f32)
    m2 = jnp.dot(at_ref[...], wba_ref[...], preferred_element_type=f32)
    merged = (jax.nn.sigmoid(gs_ref[...].astype(f32)) * m1
              + jax.nn.sigmoid(ga_ref[...].astype(f32)) * m2)
    out = h_ref[...] + jnp.dot(merged.astype(bf16), wout_ref[...], preferred_element_type=f32)
    rows = pl.program_id(0) * TM + lax.broadcasted_iota(jnp.int32, (TM, 1), 0)
    o_ref[...] = jnp.where(rows >= PAD, out, 0.0)


def _merge(h, yf, yb, proj, attn, d, wglu, wbs, wba, wout):
    row = lambda w, col: pl.BlockSpec((TM, w), lambda i: (i, col))
    full = lambda r, c: pl.BlockSpec((r, c), lambda i: (0, 0))
    return pl.pallas_call(
        _merge_kernel,
        grid=(LP // TM,),
        in_specs=[
            row(D_MODEL, 0), row(SSM_WIDTH, 0), row(SSM_WIDTH, 0),
            row(SSM_WIDTH, COL_U // SSM_WIDTH), row(ATTN_Q, 0),
            row(D_MODEL, COL_GS // D_MODEL), row(D_MODEL, COL_GA // D_MODEL),
            full(1, SSM_WIDTH), full(SSM_WIDTH, SSM_WIDTH), full(SSM_WIDTH, D_MODEL),
            full(ATTN_Q, D_MODEL), full(D_MODEL, D_MODEL),
        ],
        out_specs=row(D_MODEL, 0),
        out_shape=jax.ShapeDtypeStruct((LP, D_MODEL), f32),
        compiler_params=_cparams(("parallel",)),
        name="merge",
    )(h, yf, yb, proj, attn, proj, proj, d, wglu, wbs, wba, wout)


def _final_kernel(x_ref, g_ref, o_ref):
    o_ref[...] = _rmsnorm(x_ref[...], g_ref[...])


def _final_norm(h, g):
    return pl.pallas_call(
        _final_kernel,
        grid=(SEQ // BLOCK,),
        in_specs=[pl.BlockSpec((BLOCK, D_MODEL), lambda i: (i + 1, 0)),
                  pl.BlockSpec((1, D_MODEL), lambda i: (0, 0))],
        out_specs=pl.BlockSpec((BLOCK, D_MODEL), lambda i: (i, 0)),
        out_shape=jax.ShapeDtypeStruct((SEQ, D_MODEL), f32),
        compiler_params=_cparams(("parallel",)),
        name="final_norm",
    )(h, g)


def kernel(x, meta_tokens, ffn1_norm, ffn1_w_gate, ffn1_w_up, ffn1_w_down, mix_norm, w_in, ssm_lam_re, ssm_lam_im, ssm_log_dt, ssm_b_re, ssm_b_im, ssm_c_re, ssm_c_im, ssm_d, ssm_w_glu, attn_sink, w_branch_ssm, w_branch_attn, w_out, ffn2_norm, ffn2_w_gate, ffn2_w_up, ffn2_w_down, final_norm):
    assert x.shape == (1, SEQ, D_MODEL)
    h = jnp.concatenate([jnp.zeros((PAD, D_MODEL), f32), meta_tokens.astype(f32), x[0]], axis=0)

    col_scale = jnp.where(jnp.arange(IN_COLS) < ATTN_Q, HEAD_DIM ** -0.5, 1.0).astype(f32)
    w_in_b = (w_in * col_scale).astype(bf16)
    cast = lambda w: w.astype(bf16)
    f1g, f1u, f1d = cast(ffn1_w_gate), cast(ffn1_w_up), cast(ffn1_w_down)
    f2g, f2u, f2d = cast(ffn2_w_gate), cast(ffn2_w_up), cast(ffn2_w_down)
    wglu, wbs, wba, wo = cast(ssm_w_glu), cast(w_branch_ssm), cast(w_branch_attn), cast(w_out)
    bm, cm, a_vec, as_vec = _ssm_params(ssm_lam_re, ssm_lam_im, ssm_log_dt,
                                        ssm_b_re, ssm_b_im, ssm_c_re, ssm_c_im)
    bias = _attn_bias()

    for l in range(DEPTH):
        h = _ffn(h, ffn1_norm[l][None], f1g[l], f1u[l], f1d[l])
        proj = _inproj(h, mix_norm[l][None], w_in_b[l])
        attn = _attn(proj, attn_sink[l], bias)
        u = proj[:, COL_U:COL_GS]
        u2 = jnp.stack([_ssm_permute(u), _ssm_permute(u[::-1])])
        y2 = _ssm(u2, bm[l], cm[l], a_vec[l], as_vec[l])
        yf = _ssm_unpermute(y2[0])
        yb = _ssm_unpermute(y2[1])[::-1]
        h = _merge(h, yf, yb, proj, attn, ssm_d[l][None], wglu[l], wbs[l], wba[l], wo[l])
        h = _ffn(h, ffn2_norm[l][None], f2g[l], f2u[l], f2d[l])
    return _final_norm(h, final_norm[None])[None]
```

```python
import functools
import math

import numpy as np
import jax
import jax.numpy as jnp
from jax import lax
from jax.experimental import pallas as pl
from jax.experimental.pallas import tpu as pltpu

f32 = jnp.float32
bf16 = jnp.bfloat16

D_MODEL = 1024
SEQ = 16384
DEPTH = 4
N_META = 16
N_HEADS = 16
N_KV_HEADS = 4
HEAD_DIM = 64
Q_GROUP = N_HEADS // N_KV_HEADS
WINDOW = 128
BLOCK = 128
SSM_WIDTH = 512
SSM_GROUP_CH = 16
SSM_GROUPS = 32
SSM_STATE = 64
D_FF = 2816
EPS = 1e-6
NEG = -1e30
LOG2E = math.log2(math.e)

ATTN_Q = N_HEADS * HEAD_DIM
ATTN_KV = N_KV_HEADS * HEAD_DIM
IN_COLS = 4096
COL_K = ATTN_Q
COL_V = ATTN_Q + ATTN_KV
COL_U = ATTN_Q + 2 * ATTN_KV
COL_GS = COL_U + SSM_WIDTH
COL_GA = COL_GS + D_MODEL

PAD = BLOCK - N_META
LP = PAD + N_META + SEQ
NB = LP // BLOCK

LANES = 128
SUBLANES = 8
VMEM_LIMIT = 56 * 1024 * 1024

TM = 688
TF = D_FF // 2
TN_IN = 2048

SEG = 172
T_SSM = SUBLANES * SEG
NT_SSM = LP // T_SSM
N_SLAB = SSM_WIDTH // LANES
G_SLAB = SSM_GROUPS // N_SLAB
SLAB_STATE = G_SLAB * SSM_STATE


def _cparams(sem):
    return pltpu.CompilerParams(dimension_semantics=sem, vmem_limit_bytes=VMEM_LIMIT)


def _rmsnorm(x, g):
    ms = jnp.mean(x * x, axis=-1, keepdims=True)
    return x * lax.rsqrt(ms + EPS) * g


def _ffn_kernel(x_ref, g_ref, wg_ref, wu_ref, wd_ref, o_ref, xn_ref, acc_ref):
    j = pl.program_id(1)

    @pl.when(j == 0)
    def _():
        xn_ref[...] = _rmsnorm(x_ref[...], g_ref[...]).astype(bf16)
        acc_ref[...] = jnp.zeros_like(acc_ref)

    xn = xn_ref[...]
    a = jnp.dot(xn, wg_ref[...], preferred_element_type=f32)
    b = jnp.dot(xn, wu_ref[...], preferred_element_type=f32)
    mid = (a * jax.nn.sigmoid(a) * b).astype(bf16)
    acc_ref[...] += jnp.dot(mid, wd_ref[...], preferred_element_type=f32)

    @pl.when(j == pl.num_programs(1) - 1)
    def _():
        o_ref[...] = x_ref[...] + 0.5 * acc_ref[...]


def _ffn(h, g, wg, wu, wd):
    return pl.pallas_call(
        _ffn_kernel,
        grid=(LP // TM, D_FF // TF),
        in_specs=[
            pl.BlockSpec((TM, D_MODEL), lambda i, j: (i, 0)),
            pl.BlockSpec((1, D_MODEL), lambda i, j: (0, 0)),
            pl.BlockSpec((D_MODEL, TF), lambda i, j: (0, j)),
            pl.BlockSpec((D_MODEL, TF), lambda i, j: (0, j)),
            pl.BlockSpec((TF, D_MODEL), lambda i, j: (j, 0)),
        ],
        out_specs=pl.BlockSpec((TM, D_MODEL), lambda i, j: (i, 0)),
        out_shape=jax.ShapeDtypeStruct((LP, D_MODEL), f32),
        scratch_shapes=[pltpu.VMEM((TM, D_MODEL), bf16), pltpu.VMEM((TM, D_MODEL), f32)],
        compiler_params=_cparams(("parallel", "arbitrary")),
        name="ffn",
    )(h, g, wg, wu, wd)


def _inproj_kernel(x_ref, g_ref, w_ref, o_ref, us_ref, xn_ref):
    j = pl.program_id(1)

    @pl.when(j == 0)
    def _():
        xn_ref[...] = _rmsnorm(x_ref[...], g_ref[...]).astype(bf16)

    res = jnp.dot(xn_ref[...], w_ref[...], preferred_element_type=f32)
    o_ref[...] = res.astype(bf16)

    @pl.when(j == COL_U // TN_IN)
    def _():
        for s in range(N_SLAB):
            c0 = COL_U % TN_IN + s * LANES
            us_ref[s] = res[:, c0:c0 + LANES]


def _inproj(h, g, w):
    return pl.pallas_call(
        _inproj_kernel,
        grid=(LP // TM, IN_COLS // TN_IN),
        in_specs=[
            pl.BlockSpec((TM, D_MODEL), lambda i, j: (i, 0)),
            pl.BlockSpec((1, D_MODEL), lambda i, j: (0, 0)),
            pl.BlockSpec((D_MODEL, TN_IN), lambda i, j: (0, j)),
        ],
        out_specs=[pl.BlockSpec((TM, TN_IN), lambda i, j: (i, j)),
                   pl.BlockSpec((N_SLAB, TM, LANES), lambda i, j: (0, i, 0))],
        out_shape=[jax.ShapeDtypeStruct((LP, IN_COLS), bf16),
                   jax.ShapeDtypeStruct((N_SLAB, LP, LANES), f32)],
        scratch_shapes=[pltpu.VMEM((TM, D_MODEL), bf16)],
        compiler_params=_cparams(("parallel", "arbitrary")),
        name="inproj",
    )(h, g, w)


def _attn_kernel(sink_ref, q_ref, kp_ref, kc_ref, kn_ref, km_ref,
                 vp_ref, vc_ref, vn_ref, vm_ref, bias_ref, o_ref):
    k_all, v_all = [], []
    for kh in range(N_KV_HEADS):
        sl = slice(kh * HEAD_DIM, (kh + 1) * HEAD_DIM)
        k_all.append(jnp.concatenate([kp_ref[:, sl], kc_ref[:, sl], kn_ref[:, sl], km_ref[:, sl]], axis=0))
        v_all.append(jnp.concatenate([vp_ref[:, sl], vc_ref[:, sl], vn_ref[:, sl], vm_ref[:, sl]], axis=0))

    def logits(h):
        qh = q_ref[:, h * HEAD_DIM:(h + 1) * HEAD_DIM]
        return lax.dot_general(qh, k_all[h // Q_GROUP], (((1,), (1,)), ((), ())),
                               preferred_element_type=f32) + bias_ref[0, h]

    outs = []
    s_next = logits(0)
    for h in range(N_HEADS):
        s = s_next
        if h + 1 < N_HEADS:
            s_next = logits(h + 1)
        sink = sink_ref[h] * LOG2E
        m = jnp.maximum(jnp.max(s, axis=-1, keepdims=True), sink)
        p = jnp.exp2(s - m)
        denom = jnp.sum(p, axis=-1, keepdims=True) + jnp.exp2(sink - m)
        o = jnp.dot(p.astype(bf16), v_all[h // Q_GROUP], preferred_element_type=f32)
        outs.append(o / denom)
    o_ref[...] = jnp.concatenate(outs, axis=1).astype(bf16)


def _bias_variant(n):
    return jnp.where(n == 0, 0, jnp.where(n == 1, 1, jnp.where(n == NB - 1, 3, 2)))


def _attn(proj, sink, bias):
    kv = lambda col, fn: pl.BlockSpec((BLOCK, ATTN_KV), lambda n: (fn(n), col))
    prev = lambda n: jnp.maximum(n - 1, 0)
    cur = lambda n: n
    nxt = lambda n: jnp.minimum(n + 1, NB - 1)
    first = lambda n: 0
    ck, cv = COL_K // ATTN_KV, COL_V // ATTN_KV
    return pl.pallas_call(
        _attn_kernel,
        grid=(NB,),
        in_specs=[
            pl.BlockSpec(memory_space=pltpu.SMEM),
            pl.BlockSpec((BLOCK, ATTN_Q), lambda n: (n, 0)),
            kv(ck, prev), kv(ck, cur), kv(ck, nxt), kv(ck, first),
            kv(cv, prev), kv(cv, cur), kv(cv, nxt), kv(cv, first),
            pl.BlockSpec((1, N_HEADS, BLOCK, 4 * BLOCK), lambda n: (_bias_variant(n), 0, 0, 0)),
        ],
        out_specs=pl.BlockSpec((BLOCK, ATTN_Q), lambda n: (n, 0)),
        out_shape=jax.ShapeDtypeStruct((LP, ATTN_Q), bf16),
        compiler_params=_cparams(("parallel",)),
        name="attn",
    )(sink, proj, proj, proj, proj, proj, proj, proj, proj, proj, bias)


def _attn_bias():
    qi = np.arange(BLOCK)[:, None]
    sj = np.arange(3 * BLOCK)[None, :]
    dist = np.abs(qi + BLOCK - sj)
    slopes = 2.0 ** (-8.0 * np.arange(1, N_HEADS + 1) / N_HEADS)
    band = -LOG2E * slopes[:, None, None] * dist[None].astype(np.float64)
    in_win = (dist <= WINDOW)[None]
    meta = np.where(np.arange(BLOCK) >= PAD, 0.0, NEG)[None, None, :]
    meta = np.broadcast_to(meta, (N_HEADS, BLOCK, BLOCK))
    out = []
    for blk_ok in ((False, False, True), (False, True, True), (True, True, True), (True, True, False)):
        kvalid = np.repeat(np.asarray(blk_ok), BLOCK)[None, None, :]
        b = np.where(in_win & kvalid, band, NEG)
        out.append(np.concatenate([b, meta], axis=-1))
    return jnp.asarray(np.stack(out), dtype=f32)


def _cmul(ar, ai, br, bi):
    return ar * br - ai * bi, ar * bi + ai * br


def _ssm_direction(reverse, u_ref, bm_ref, cm_ref, a_ref, as_ref, y_ref,
                   up_ref, bu_ref, xs_ref, yp_ref, carry_ref):
    d = 1 if reverse else 0
    row = lax.broadcasted_iota(jnp.int32, (SUBLANES, SLAB_STATE), 0)
    bcast = lambda v: jnp.broadcast_to(v, (SUBLANES, SLAB_STATE))

    def rows8(j):
        return pl.ds(pl.multiple_of(j * SUBLANES, SUBLANES), SUBLANES)

    def slab_body(s, _):
        def gather(j, _):
            up_ref[rows8(j), :] = u_ref[s, pl.ds(j, SUBLANES, stride=SEG), :]
            return 0

        lax.fori_loop(0, SEG, gather, 0)
        bu_ref[...] = jnp.dot(up_ref[...].astype(bf16), bm_ref[d, s], preferred_element_type=f32)
        a = a_ref[d, s]
        are, aim = bcast(a[0]), bcast(a[1])

        def step(k, x):
            j = SEG - 1 - k if reverse else k
            blk = bu_ref[rows8(j), :]
            pr, pi = _cmul(are, aim, x[0], x[1])
            return pr + blk[:, :SLAB_STATE], pi + blk[:, SLAB_STATE:]

        zero = jnp.zeros((SUBLANES, SLAB_STATE), f32)
        er, ei = lax.fori_loop(0, SEG, step, (zero, zero))

        asg = as_ref[d, s]
        asr, asi = bcast(asg[0]), bcast(asg[1])
        cr, ci = carry_ref[s, 0], carry_ref[s, 1]
        cin_r, cin_i = cr, ci
        order = range(SUBLANES - 1, -1, -1) if reverse else range(SUBLANES)
        for n, r in enumerate(order):
            if n > 0:
                cin_r = jnp.where(row == r, cr, cin_r)
                cin_i = jnp.where(row == r, ci, cin_i)
            pr, pi = _cmul(asr, asi, cr, ci)
            cr = pr + bcast(er[r:r + 1])
            ci = pi + bcast(ei[r:r + 1])
        carry_ref[s, 0] = cr
        carry_ref[s, 1] = ci

        def step2(k, x):
            xr, xi = step(k, x)
            j = SEG - 1 - k if reverse else k
            xs_ref[rows8(j), :SLAB_STATE] = xr
            xs_ref[rows8(j), SLAB_STATE:] = xi
            return xr, xi

        lax.fori_loop(0, SEG, step2, (cin_r, cin_i))
        yp_ref[...] = jnp.dot(xs_ref[...].astype(bf16), cm_ref[d, s], preferred_element_type=f32)

        def scatter(j, _):
            y_ref[s, pl.ds(j, SUBLANES, stride=SEG), :] = yp_ref[rows8(j), :]
            return 0

        lax.fori_loop(0, SEG, scatter, 0)
        return 0

    lax.fori_loop(0, N_SLAB, slab_body, 0)


def _ssm_kernel(uf_ref, ub_ref, bm_ref, cm_ref, a_ref, as_ref, yf_ref, yb_ref,
                up_ref, bu_ref, xs_ref, yp_ref, carry_ref):
    @pl.when(pl.program_id(0) == 0)
    def _():
        carry_ref[...] = jnp.zeros_like(carry_ref)

    scratch = (up_ref, bu_ref, xs_ref, yp_ref)
    _ssm_direction(False, uf_ref, bm_ref, cm_ref, a_ref, as_ref, yf_ref, *scratch, carry_ref.at[0])
    _ssm_direction(True, ub_ref, bm_ref, cm_ref, a_ref, as_ref, yb_ref, *scratch, carry_ref.at[1])


def _ssm(us, bm, cm, a, asg):
    fwd = pl.BlockSpec((N_SLAB, T_SSM, LANES), lambda i: (0, i, 0))
    bwd = pl.BlockSpec((N_SLAB, T_SSM, LANES), lambda i: (0, NT_SSM - 1 - i, 0))
    const = lambda *shape: pl.BlockSpec(shape, lambda i: (0,) * len(shape))
    out = jax.ShapeDtypeStruct((N_SLAB, LP, LANES), f32)
    return pl.pallas_call(
        _ssm_kernel,
        grid=(NT_SSM,),
        in_specs=[
            fwd, bwd,
            const(2, N_SLAB, LANES, 2 * SLAB_STATE),
            const(2, N_SLAB, 2 * SLAB_STATE, LANES),
            const(2, N_SLAB, 2, 1, SLAB_STATE),
            const(2, N_SLAB, 2, 1, SLAB_STATE),
        ],
        out_specs=[fwd, bwd],
        out_shape=[out, out],
        scratch_shapes=[
            pltpu.VMEM((T_SSM, LANES), f32),
            pltpu.VMEM((T_SSM, 2 * SLAB_STATE), f32),
            pltpu.VMEM((T_SSM, 2 * SLAB_STATE), f32),
            pltpu.VMEM((T_SSM, LANES), f32),
            pltpu.VMEM((2, N_SLAB, 2, SUBLANES, SLAB_STATE), f32),
        ],
        compiler_params=_cparams(("arbitrary",)),
        name="ssm",
    )(us, us, bm, cm, a, asg)


def _ssm_params(lam_re, lam_im, log_dt, b_re, b_im, c_re, c_im):
    dt = jnp.exp(log_dt)[..., None]
    mag = jnp.exp(lam_re * dt)
    ar, ai = mag * jnp.cos(lam_im * dt), mag * jnp.sin(lam_im * dt)
    mag_s = jnp.exp(lam_re * dt * SEG)
    sr, si = mag_s * jnp.cos(lam_im * dt * SEG), mag_s * jnp.sin(lam_im * dt * SEG)
    den = lam_re * lam_re + lam_im * lam_im
    fr = ((ar - 1.0) * lam_re + ai * lam_im) / den
    fi = (ai * lam_re - (ar - 1.0) * lam_im) / den
    bbr = fr[..., None] * b_re - fi[..., None] * b_im
    bbi = fr[..., None] * b_im + fi[..., None] * b_re
    eye = jnp.eye(G_SLAB, dtype=f32)

    def bmat(b):
        b = b.reshape(DEPTH, 2, N_SLAB, G_SLAB, SSM_STATE, SSM_GROUP_CH)
        m = jnp.einsum('ldsgpi,gh->ldsgihp', b, eye)
        return m.reshape(DEPTH, 2, N_SLAB, LANES, SLAB_STATE)

    def cmat(c):
        c = c.reshape(DEPTH, 2, N_SLAB, G_SLAB, SSM_GROUP_CH, SSM_STATE)
        m = jnp.einsum('ldsgop,gh->ldshpgo', c, eye)
        return m.reshape(DEPTH, 2, N_SLAB, SLAB_STATE, LANES)

    bm = jnp.concatenate([bmat(bbr), bmat(bbi)], axis=-1).astype(bf16)
    cm = jnp.concatenate([cmat(c_re), -cmat(c_im)], axis=-2).astype(bf16)

    def vec(r, i):
        v = jnp.stack([r, i], axis=2)
        v = v.reshape(DEPTH, 2, 2, N_SLAB, 1, SLAB_STATE)
        return v.transpose(0, 1, 3, 2, 4, 5)

    return bm, cm, vec(ar, ai), vec(sr, si)


def _gelu_tanh(x):
    c = math.sqrt(2.0 / math.pi)
    return 0.5 * x * (1.0 + jnp.tanh(c * (x + 0.044715 * (x * x * x))))


def _merge_kernel(h_ref, yf_ref, yb_ref, us_ref, at_ref, gs_ref, ga_ref, d_ref,
                  wglu_ref, wbs_ref, wba_ref, wout_ref, o_ref):
    lanes = lambda ref: jnp.concatenate([ref[s] for s in range(N_SLAB)], axis=1)
    y = lanes(yf_ref) + lanes(yb_ref) + d_ref[...] * lanes(us_ref)
    z = _gelu_tanh(y)
    gl = jnp.dot(z.astype(bf16), wglu_ref[...], preferred_element_type=f32)
    ys = (z * jax.nn.sigmoid(gl)).astype(bf16)
    m1 = jnp.dot(ys, wbs_ref[...], preferred_element_type=f32)
    m2 = jnp.dot(at_ref[...], wba_ref[...], preferred_element_type=f32)
    merged = (jax.nn.sigmoid(gs_ref[...].astype(f32)) * m1
              + jax.nn.sigmoid(ga_ref[...].astype(f32)) * m2)
    out = h_ref[...] + jnp.dot(merged.astype(bf16), wout_ref[...], preferred_element_type=f32)
    rows = pl.program_id(0) * TM + lax.broadcasted_iota(jnp.int32, (TM, 1), 0)
    o_ref[...] = jnp.where(rows >= PAD, out, 0.0)


def _merge(h, yf, yb, us, proj, attn, d, wglu, wbs, wba, wout):
    row = lambda w, col: pl.BlockSpec((TM, w), lambda i: (i, col))
    slab = pl.BlockSpec((N_SLAB, TM, LANES), lambda i: (0, i, 0))
    full = lambda r, c: pl.BlockSpec((r, c), lambda i: (0, 0))
    return pl.pallas_call(
        _merge_kernel,
        grid=(LP // TM,),
        in_specs=[
            row(D_MODEL, 0), slab, slab, slab, row(ATTN_Q, 0),
            row(D_MODEL, COL_GS // D_MODEL), row(D_MODEL, COL_GA // D_MODEL),
            full(1, SSM_WIDTH), full(SSM_WIDTH, SSM_WIDTH), full(SSM_WIDTH, D_MODEL),
            full(ATTN_Q, D_MODEL), full(D_MODEL, D_MODEL),
        ],
        out_specs=row(D_MODEL, 0),
        out_shape=jax.ShapeDtypeStruct((LP, D_MODEL), f32),
        compiler_params=_cparams(("parallel",)),
        name="merge",
    )(h, yf, yb, us, attn, proj, proj, d, wglu, wbs, wba, wout)


def _final_kernel(x_ref, g_ref, o_ref):
    o_ref[...] = _rmsnorm(x_ref[...], g_ref[...])


def _final_norm(h, g):
    return pl.pallas_call(
        _final_kernel,
        grid=(SEQ // BLOCK,),
        in_specs=[pl.BlockSpec((BLOCK, D_MODEL), lambda i: (i + 1, 0)),
                  pl.BlockSpec((1, D_MODEL), lambda i: (0, 0))],
        out_specs=pl.BlockSpec((BLOCK, D_MODEL), lambda i: (i, 0)),
        out_shape=jax.ShapeDtypeStruct((SEQ, D_MODEL), f32),
        compiler_params=_cparams(("parallel",)),
        name="final_norm",
    )(h, g)


def kernel(x, meta_tokens, ffn1_norm, ffn1_w_gate, ffn1_w_up, ffn1_w_down, mix_norm, w_in, ssm_lam_re, ssm_lam_im, ssm_log_dt, ssm_b_re, ssm_b_im, ssm_c_re, ssm_c_im, ssm_d, ssm_w_glu, attn_sink, w_branch_ssm, w_branch_attn, w_out, ffn2_norm, ffn2_w_gate, ffn2_w_up, ffn2_w_down, final_norm):
    assert x.shape == (1, SEQ, D_MODEL)
    h = jnp.concatenate([jnp.zeros((PAD, D_MODEL), f32), meta_tokens.astype(f32), x[0]], axis=0)

    col_scale = jnp.where(jnp.arange(IN_COLS) < ATTN_Q, LOG2E * HEAD_DIM ** -0.5, 1.0).astype(f32)
    w_in_b = (w_in * col_scale).astype(bf16)
    cast = lambda w: w.astype(bf16)
    f1g, f1u, f1d = cast(ffn1_w_gate), cast(ffn1_w_up), cast(ffn1_w_down)
    f2g, f2u, f2d = cast(ffn2_w_gate), cast(ffn2_w_up), cast(ffn2_w_down)
    wglu, wbs, wba, wo = cast(ssm_w_glu), cast(w_branch_ssm), cast(w_branch_attn), cast(w_out)
    bm, cm, a_vec, as_vec = _ssm_params(ssm_lam_re, ssm_lam_im, ssm_log_dt,
                                        ssm_b_re, ssm_b_im, ssm_c_re, ssm_c_im)
    bias = _attn_bias()

    for l in range(DEPTH):
        h = _ffn(h, ffn1_norm[l][None], f1g[l], f1u[l], f1d[l])
        proj, us = _inproj(h, mix_norm[l][None], w_in_b[l])
        attn = _attn(proj, attn_sink[l], bias)
        yf, yb = _ssm(us, bm[l], cm[l], a_vec[l], as_vec[l])
        h = _merge(h, yf, yb, us, proj, attn, ssm_d[l][None], wglu[l], wbs[l], wba[l], wo[l])
        h = _ffn(h, ffn2_norm[l][None], f2g[l], f2u[l], f2d[l])
    return _final_norm(h, final_norm[None])[None]
```

```python
import functools
import math

import numpy as np
import jax
import jax.numpy as jnp
from jax import lax
from jax.experimental import pallas as pl
from jax.experimental.pallas import tpu as pltpu

f32 = jnp.float32
bf16 = jnp.bfloat16

D_MODEL = 1024
SEQ = 16384
DEPTH = 4
N_META = 16
N_HEADS = 16
N_KV_HEADS = 4
HEAD_DIM = 64
Q_GROUP = N_HEADS // N_KV_HEADS
WINDOW = 128
BLOCK = 128
SSM_WIDTH = 512
SSM_GROUP_CH = 16
SSM_GROUPS = 32
SSM_STATE = 64
D_FF = 2816
EPS = 1e-6
NEG = -1e30
LOG2E = math.log2(math.e)

ATTN_Q = N_HEADS * HEAD_DIM
ATTN_KV = N_KV_HEADS * HEAD_DIM
IN_COLS = 4096
COL_K = ATTN_Q
COL_V = ATTN_Q + ATTN_KV
COL_U = ATTN_Q + 2 * ATTN_KV
COL_GS = COL_U + SSM_WIDTH
COL_GA = COL_GS + D_MODEL

PAD = BLOCK - N_META
LP = PAD + N_META + SEQ
NB = LP // BLOCK

LANES = 128
SUBLANES = 8
VMEM_LIMIT = 56 * 1024 * 1024

TM = 688
TF = D_FF // 2
TN_IN = 2048
T_FINAL = 1024

SEG = 172
T_SSM = SUBLANES * SEG
NT_SSM = LP // T_SSM
N_SLAB = SSM_WIDTH // LANES
G_SLAB = SSM_GROUPS // N_SLAB
SLAB_STATE = G_SLAB * SSM_STATE
SCAN_UNROLL = 4


def _cparams(sem):
    return pltpu.CompilerParams(dimension_semantics=sem, vmem_limit_bytes=VMEM_LIMIT)


def _rmsnorm(x, g):
    ms = jnp.mean(x * x, axis=-1, keepdims=True)
    return x * lax.rsqrt(ms + EPS) * g


def _ffn_kernel(x_ref, g_ref, wg_ref, wu_ref, wd_ref, o_ref, xn_ref, acc_ref):
    j = pl.program_id(1)

    @pl.when(j == 0)
    def _():
        xn_ref[...] = _rmsnorm(x_ref[...], g_ref[...]).astype(bf16)
        acc_ref[...] = jnp.zeros_like(acc_ref)

    xn = xn_ref[...]
    a = jnp.dot(xn, wg_ref[...], preferred_element_type=f32)
    b = jnp.dot(xn, wu_ref[...], preferred_element_type=f32)
    mid = (a * jax.nn.sigmoid(a) * b).astype(bf16)
    acc_ref[...] += jnp.dot(mid, wd_ref[...], preferred_element_type=f32)

    @pl.when(j == pl.num_programs(1) - 1)
    def _():
        o_ref[...] = x_ref[...] + 0.5 * acc_ref[...]


def _ffn(l, h, g, wg, wu, wd):
    return pl.pallas_call(
        _ffn_kernel,
        grid=(LP // TM, D_FF // TF),
        in_specs=[
            pl.BlockSpec((TM, D_MODEL), lambda i, j: (i, 0)),
            pl.BlockSpec((None, 1, D_MODEL), lambda i, j: (l, 0, 0)),
            pl.BlockSpec((None, D_MODEL, TF), lambda i, j: (l, 0, j)),
            pl.BlockSpec((None, D_MODEL, TF), lambda i, j: (l, 0, j)),
            pl.BlockSpec((None, TF, D_MODEL), lambda i, j: (l, j, 0)),
        ],
        out_specs=pl.BlockSpec((TM, D_MODEL), lambda i, j: (i, 0)),
        out_shape=jax.ShapeDtypeStruct((LP, D_MODEL), f32),
        scratch_shapes=[pltpu.VMEM((TM, D_MODEL), bf16), pltpu.VMEM((TM, D_MODEL), f32)],
        compiler_params=_cparams(("parallel", "arbitrary")),
        name="ffn",
    )(h, g, wg, wu, wd)


def _inproj_kernel(x_ref, g_ref, w_ref, o_ref, us_ref, xn_ref):
    j = pl.program_id(1)

    @pl.when(j == 0)
    def _():
        xn_ref[...] = _rmsnorm(x_ref[...], g_ref[...]).astype(bf16)

    res = jnp.dot(xn_ref[...], w_ref[...], preferred_element_type=f32)
    o_ref[...] = res.astype(bf16)

    @pl.when(j == COL_U // TN_IN)
    def _():
        for s in range(N_SLAB):
            c0 = COL_U % TN_IN + s * LANES
            us_ref[s] = res[:, c0:c0 + LANES]


def _inproj(l, h, g, w):
    return pl.pallas_call(
        _inproj_kernel,
        grid=(LP // TM, IN_COLS // TN_IN),
        in_specs=[
            pl.BlockSpec((TM, D_MODEL), lambda i, j: (i, 0)),
            pl.BlockSpec((None, 1, D_MODEL), lambda i, j: (l, 0, 0)),
            pl.BlockSpec((None, D_MODEL, TN_IN), lambda i, j: (l, 0, j)),
        ],
        out_specs=[pl.BlockSpec((TM, TN_IN), lambda i, j: (i, j)),
                   pl.BlockSpec((N_SLAB, TM, LANES), lambda i, j: (0, i, 0))],
        out_shape=[jax.ShapeDtypeStruct((LP, IN_COLS), bf16),
                   jax.ShapeDtypeStruct((N_SLAB, LP, LANES), f32)],
        scratch_shapes=[pltpu.VMEM((TM, D_MODEL), bf16)],
        compiler_params=_cparams(("parallel", "arbitrary")),
        name="inproj",
    )(h, g, w)


def _attn_kernel(sink_ref, q_ref, kp_ref, kc_ref, kn_ref, km_ref,
                 vp_ref, vc_ref, vn_ref, vm_ref, bias_ref, o_ref):
    k_all, v_all = [], []
    for kh in range(N_KV_HEADS):
        sl = slice(kh * HEAD_DIM, (kh + 1) * HEAD_DIM)
        k_all.append(jnp.concatenate([kp_ref[:, sl], kc_ref[:, sl], kn_ref[:, sl], km_ref[:, sl]], axis=0))
        v_all.append(jnp.concatenate([vp_ref[:, sl], vc_ref[:, sl], vn_ref[:, sl], vm_ref[:, sl]], axis=0))

    def logits(h):
        qh = q_ref[:, h * HEAD_DIM:(h + 1) * HEAD_DIM]
        return lax.dot_general(qh, k_all[h // Q_GROUP], (((1,), (1,)), ((), ())),
                               preferred_element_type=f32) + bias_ref[0, h]

    def softmax(h, s):
        sink = sink_ref[h] * LOG2E
        m = jnp.maximum(jnp.max(s, axis=-1, keepdims=True), sink)
        p = jnp.exp2(s - m)
        denom = jnp.sum(p, axis=-1, keepdims=True) + jnp.exp2(sink - m)
        return p.astype(bf16), denom

    outs = []
    s_q = {0: logits(0), 1: logits(1)}
    p_q = {0: softmax(0, s_q.pop(0))}
    for h in range(N_HEADS):
        if h + 2 < N_HEADS:
            s_q[h + 2] = logits(h + 2)
        if h + 1 < N_HEADS:
            p_q[h + 1] = softmax(h + 1, s_q.pop(h + 1))
        p, denom = p_q.pop(h)
        o = jnp.dot(p, v_all[h // Q_GROUP], preferred_element_type=f32)
        outs.append(o / denom)
    o_ref[...] = jnp.concatenate(outs, axis=1).astype(bf16)


def _bias_variant(n):
    return jnp.where(n == 0, 0, jnp.where(n == 1, 1, jnp.where(n == NB - 1, 3, 2)))


def _attn(proj, sink, bias):
    kv = lambda col, fn: pl.BlockSpec((BLOCK, ATTN_KV), lambda n: (fn(n), col))
    prev = lambda n: jnp.maximum(n - 1, 0)
    cur = lambda n: n
    nxt = lambda n: jnp.minimum(n + 1, NB - 1)
    first = lambda n: 0
    ck, cv = COL_K // ATTN_KV, COL_V // ATTN_KV
    return pl.pallas_call(
        _attn_kernel,
        grid=(NB,),
        in_specs=[
            pl.BlockSpec(memory_space=pltpu.SMEM),
            pl.BlockSpec((BLOCK, ATTN_Q), lambda n: (n, 0)),
            kv(ck, prev), kv(ck, cur), kv(ck, nxt), kv(ck, first),
            kv(cv, prev), kv(cv, cur), kv(cv, nxt), kv(cv, first),
            pl.BlockSpec((1, N_HEADS, BLOCK, 4 * BLOCK), lambda n: (_bias_variant(n), 0, 0, 0)),
        ],
        out_specs=pl.BlockSpec((BLOCK, ATTN_Q), lambda n: (n, 0)),
        out_shape=jax.ShapeDtypeStruct((LP, ATTN_Q), bf16),
        compiler_params=_cparams(("parallel",)),
        name="attn",
    )(sink, proj, proj, proj, proj, proj, proj, proj, proj, proj, bias)


def _attn_bias():
    qi = np.arange(BLOCK)[:, None]
    sj = np.arange(3 * BLOCK)[None, :]
    dist = np.abs(qi + BLOCK - sj)
    slopes = 2.0 ** (-8.0 * np.arange(1, N_HEADS + 1) / N_HEADS)
    band = -LOG2E * slopes[:, None, None] * dist[None].astype(np.float64)
    in_win = (dist <= WINDOW)[None]
    meta = np.where(np.arange(BLOCK) >= PAD, 0.0, NEG)[None, None, :]
    meta = np.broadcast_to(meta, (N_HEADS, BLOCK, BLOCK))
    out = []
    for blk_ok in ((False, False, True), (False, True, True), (True, True, True), (True, True, False)):
        kvalid = np.repeat(np.asarray(blk_ok), BLOCK)[None, None, :]
        b = np.where(in_win & kvalid, band, NEG)
        out.append(np.concatenate([b, meta], axis=-1))
    return jnp.asarray(np.stack(out), dtype=f32)


def _cmul(ar, ai, br, bi):
    return ar * br - ai * bi, ar * bi + ai * br


def _ssm_direction(reverse, u_ref, bm_ref, cm_ref, a_ref, as_ref, y_ref,
                   up_ref, bu_ref, xs_ref, yp_ref, carry_ref):
    d = 1 if reverse else 0
    row = lax.broadcasted_iota(jnp.int32, (SUBLANES, SLAB_STATE), 0)
    bcast = lambda v: jnp.broadcast_to(v, (SUBLANES, SLAB_STATE))

    def rows8(j):
        return pl.ds(pl.multiple_of(j * SUBLANES, SUBLANES), SUBLANES)

    def slab_body(s, _):
        def gather(j, _):
            up_ref[rows8(j), :] = u_ref[s, pl.ds(j, SUBLANES, stride=SEG), :]
            return 0

        lax.fori_loop(0, SEG, gather, 0, unroll=SCAN_UNROLL)
        bu_ref[...] = jnp.dot(up_ref[...].astype(bf16), bm_ref[d, s], preferred_element_type=f32)
        a = a_ref[d, s]
        are, aim = bcast(a[0]), bcast(a[1])

        def step(k, x):
            j = SEG - 1 - k if reverse else k
            blk = bu_ref[rows8(j), :]
            pr, pi = _cmul(are, aim, x[0], x[1])
            return pr + blk[:, :SLAB_STATE], pi + blk[:, SLAB_STATE:]

        zero = jnp.zeros((SUBLANES, SLAB_STATE), f32)
        er, ei = lax.fori_loop(0, SEG, step, (zero, zero), unroll=SCAN_UNROLL)

        asg = as_ref[d, s]
        asr, asi = bcast(asg[0]), bcast(asg[1])
        cr, ci = carry_ref[s, 0], carry_ref[s, 1]
        cin_r, cin_i = cr, ci
        order = range(SUBLANES - 1, -1, -1) if reverse else range(SUBLANES)
        for n, r in enumerate(order):
            if n > 0:
                cin_r = jnp.where(row == r, cr, cin_r)
                cin_i = jnp.where(row == r, ci, cin_i)
            pr, pi = _cmul(asr, asi, cr, ci)
            cr = pr + bcast(er[r:r + 1])
            ci = pi + bcast(ei[r:r + 1])
        carry_ref[s, 0] = cr
        carry_ref[s, 1] = ci

        def step2(k, x):
            xr, xi = step(k, x)
            j = SEG - 1 - k if reverse else k
            xs_ref[rows8(j), :SLAB_STATE] = xr
            xs_ref[rows8(j), SLAB_STATE:] = xi
            return xr, xi

        lax.fori_loop(0, SEG, step2, (cin_r, cin_i), unroll=SCAN_UNROLL)
        yp_ref[...] = jnp.dot(xs_ref[...].astype(bf16), cm_ref[d, s], preferred_element_type=f32)

        def scatter(j, _):
            y_ref[s, pl.ds(j, SUBLANES, stride=SEG), :] = yp_ref[rows8(j), :]
            return 0

        lax.fori_loop(0, SEG, scatter, 0, unroll=SCAN_UNROLL)
        return 0

    lax.fori_loop(0, N_SLAB, slab_body, 0)


def _ssm_kernel(uf_ref, ub_ref, bm_ref, cm_ref, a_ref, as_ref, yf_ref, yb_ref,
                up_ref, bu_ref, xs_ref, yp_ref, carry_ref):
    @pl.when(pl.program_id(0) == 0)
    def _():
        carry_ref[...] = jnp.zeros_like(carry_ref)

    scratch = (up_ref, bu_ref, xs_ref, yp_ref)
    _ssm_direction(False, uf_ref, bm_ref, cm_ref, a_ref, as_ref, yf_ref, *scratch, carry_ref.at[0])
    _ssm_direction(True, ub_ref, bm_ref, cm_ref, a_ref, as_ref, yb_ref, *scratch, carry_ref.at[1])


def _ssm(l, us, bm, cm, a, asg):
    fwd = pl.BlockSpec((N_SLAB, T_SSM, LANES), lambda i: (0, i, 0))
    bwd = pl.BlockSpec((N_SLAB, T_SSM, LANES), lambda i: (0, NT_SSM - 1 - i, 0))
    const = lambda *shape: pl.BlockSpec((None,) + shape, lambda i: (l,) + (0,) * len(shape))
    out = jax.ShapeDtypeStruct((N_SLAB, LP, LANES), f32)
    return pl.pallas_call(
        _ssm_kernel,
        grid=(NT_SSM,),
        in_specs=[
            fwd, bwd,
            const(2, N_SLAB, LANES, 2 * SLAB_STATE),
            const(2, N_SLAB, 2 * SLAB_STATE, LANES),
            const(2, N_SLAB, 2, 1, SLAB_STATE),
            const(2, N_SLAB, 2, 1, SLAB_STATE),
        ],
        out_specs=[fwd, bwd],
        out_shape=[out, out],
        scratch_shapes=[
            pltpu.VMEM((T_SSM, LANES), f32),
            pltpu.VMEM((T_SSM, 2 * SLAB_STATE), f32),
            pltpu.VMEM((T_SSM, 2 * SLAB_STATE), f32),
            pltpu.VMEM((T_SSM, LANES), f32),
            pltpu.VMEM((2, N_SLAB, 2, SUBLANES, SLAB_STATE), f32),
        ],
        compiler_params=_cparams(("arbitrary",)),
        name="ssm",
    )(us, us, bm, cm, a, asg)


def _ssm_params(lam_re, lam_im, log_dt, b_re, b_im, c_re, c_im):
    dt = jnp.exp(log_dt)[..., None]
    mag = jnp.exp(lam_re * dt)
    ar, ai = mag * jnp.cos(lam_im * dt), mag * jnp.sin(lam_im * dt)
    mag_s = jnp.exp(lam_re * dt * SEG)
    sr, si = mag_s * jnp.cos(lam_im * dt * SEG), mag_s * jnp.sin(lam_im * dt * SEG)
    den = lam_re * lam_re + lam_im * lam_im
    fr = ((ar - 1.0) * lam_re + ai * lam_im) / den
    fi = (ai * lam_re - (ar - 1.0) * lam_im) / den
    bbr = fr[..., None] * b_re - fi[..., None] * b_im
    bbi = fr[..., None] * b_im + fi[..., None] * b_re
    eye = jnp.eye(G_SLAB, dtype=f32)

    def bmat(b):
        b = b.reshape(DEPTH, 2, N_SLAB, G_SLAB, SSM_STATE, SSM_GROUP_CH)
        m = jnp.einsum('ldsgpi,gh->ldsgihp', b, eye)
        return m.reshape(DEPTH, 2, N_SLAB, LANES, SLAB_STATE)

    def cmat(c):
        c = c.reshape(DEPTH, 2, N_SLAB, G_SLAB, SSM_GROUP_CH, SSM_STATE)
        m = jnp.einsum('ldsgop,gh->ldshpgo', c, eye)
        return m.reshape(DEPTH, 2, N_SLAB, SLAB_STATE, LANES)

    bm = jnp.concatenate([bmat(bbr), bmat(bbi)], axis=-1).astype(bf16)
    cm = jnp.concatenate([cmat(c_re), -cmat(c_im)], axis=-2).astype(bf16)

    def vec(r, i):
        v = jnp.stack([r, i], axis=2)
        v = v.reshape(DEPTH, 2, 2, N_SLAB, 1, SLAB_STATE)
        return v.transpose(0, 1, 3, 2, 4, 5)

    return bm, cm, vec(ar, ai), vec(sr, si)


def _gelu_tanh(x):
    c = math.sqrt(2.0 / math.pi)
    return 0.5 * x * (1.0 + jnp.tanh(c * (x + 0.044715 * (x * x * x))))


def _merge_kernel(h_ref, yf_ref, yb_ref, us_ref, at_ref, gs_ref, ga_ref, d_ref,
                  wglu_ref, wbs_ref, wba_ref, wout_ref, o_ref):
    lanes = lambda ref: jnp.concatenate([ref[s] for s in range(N_SLAB)], axis=1)
    y = lanes(yf_ref) + lanes(yb_ref) + d_ref[...] * lanes(us_ref)
    z = _gelu_tanh(y)
    gl = jnp.dot(z.astype(bf16), wglu_ref[...], preferred_element_type=f32)
    ys = (z * jax.nn.sigmoid(gl)).astype(bf16)
    m1 = jnp.dot(ys, wbs_ref[...], preferred_element_type=f32)
    m2 = jnp.dot(at_ref[...], wba_ref[...], preferred_element_type=f32)
    merged = (jax.nn.sigmoid(gs_ref[...].astype(f32)) * m1
              + jax.nn.sigmoid(ga_ref[...].astype(f32)) * m2)
    out = h_ref[...] + jnp.dot(merged.astype(bf16), wout_ref[...], preferred_element_type=f32)
    rows = pl.program_id(0) * TM + lax.broadcasted_iota(jnp.int32, (TM, 1), 0)
    o_ref[...] = jnp.where(rows >= PAD, out, 0.0)


def _merge(l, h, yf, yb, us, proj, attn, d, wglu, wbs, wba, wout):
    row = lambda w, col: pl.BlockSpec((TM, w), lambda i: (i, col))
    slab = pl.BlockSpec((N_SLAB, TM, LANES), lambda i: (0, i, 0))
    full = lambda r, c: pl.BlockSpec((None, r, c), lambda i: (l, 0, 0))
    return pl.pallas_call(
        _merge_kernel,
        grid=(LP // TM,),
        in_specs=[
            row(D_MODEL, 0), slab, slab, slab, row(ATTN_Q, 0),
            row(D_MODEL, COL_GS // D_MODEL), row(D_MODEL, COL_GA // D_MODEL),
            full(1, SSM_WIDTH), full(SSM_WIDTH, SSM_WIDTH), full(SSM_WIDTH, D_MODEL),
            full(ATTN_Q, D_MODEL), full(D_MODEL, D_MODEL),
        ],
        out_specs=row(D_MODEL, 0),
        out_shape=jax.ShapeDtypeStruct((LP, D_MODEL), f32),
        compiler_params=_cparams(("parallel",)),
        name="merge",
    )(h, yf, yb, us, attn, proj, proj, d, wglu, wbs, wba, wout)


def _final_kernel(x_ref, g_ref, o_ref):
    o_ref[...] = _rmsnorm(x_ref[...], g_ref[...])


def _final_norm(h, g):
    return pl.pallas_call(
        _final_kernel,
        grid=(SEQ // T_FINAL,),
        in_specs=[pl.BlockSpec((pl.Element(T_FINAL), pl.Element(D_MODEL)),
                               lambda i: (pl.multiple_of((i * (T_FINAL // BLOCK) + 1) * BLOCK, BLOCK), 0)),
                  pl.BlockSpec((1, D_MODEL), lambda i: (0, 0))],
        out_specs=pl.BlockSpec((T_FINAL, D_MODEL), lambda i: (i, 0)),
        out_shape=jax.ShapeDtypeStruct((SEQ, D_MODEL), f32),
        compiler_params=_cparams(("parallel",)),
        name="final_norm",
    )(h, g)


def kernel(x, meta_tokens, ffn1_norm, ffn1_w_gate, ffn1_w_up, ffn1_w_down, mix_norm, w_in, ssm_lam_re, ssm_lam_im, ssm_log_dt, ssm_b_re, ssm_b_im, ssm_c_re, ssm_c_im, ssm_d, ssm_w_glu, attn_sink, w_branch_ssm, w_branch_attn, w_out, ffn2_norm, ffn2_w_gate, ffn2_w_up, ffn2_w_down, final_norm):
    assert x.shape == (1, SEQ, D_MODEL)
    h = jnp.concatenate([jnp.zeros((PAD, D_MODEL), f32), meta_tokens.astype(f32), x[0]], axis=0)

    col_scale = jnp.where(jnp.arange(IN_COLS) < ATTN_Q, LOG2E * HEAD_DIM ** -0.5, 1.0).astype(f32)
    w_in_b = (w_in * col_scale).astype(bf16)
    cast = lambda w: w.astype(bf16)
    f1g, f1u, f1d = cast(ffn1_w_gate), cast(ffn1_w_up), cast(ffn1_w_down)
    f2g, f2u, f2d = cast(ffn2_w_gate), cast(ffn2_w_up), cast(ffn2_w_down)
    wglu, wbs, wba, wo = cast(ssm_w_glu), cast(w_branch_ssm), cast(w_branch_attn), cast(w_out)
    bm, cm, a_vec, as_vec = _ssm_params(ssm_lam_re, ssm_lam_im, ssm_log_dt,
                                        ssm_b_re, ssm_b_im, ssm_c_re, ssm_c_im)
    bias = _attn_bias()

    n1, nm, n2, dskip = (v[:, None, :] for v in (ffn1_norm, mix_norm, ffn2_norm, ssm_d))
    for l in range(DEPTH):
        h = _ffn(l, h, n1, f1g, f1u, f1d)
        proj, us = _inproj(l, h, nm, w_in_b)
        attn = _attn(proj, attn_sink[l], bias)
        yf, yb = _ssm(l, us, bm, cm, a_vec, as_vec)
        h = _merge(l, h, yf, yb, us, proj, attn, dskip, wglu, wbs, wba, wo)
        h = _ffn(l, h, n2, f2g, f2u, f2d)
    return _final_norm(h, final_norm[None])[None]
```

```python
import functools
import math

import numpy as np
import jax
import jax.numpy as jnp
from jax import lax
from jax.experimental import pallas as pl
from jax.experimental.pallas import tpu as pltpu

f32 = jnp.float32
bf16 = jnp.bfloat16

D_MODEL = 1024
SEQ = 16384
DEPTH = 4
N_META = 16
N_HEADS = 16
N_KV_HEADS = 4
HEAD_DIM = 64
Q_GROUP = N_HEADS // N_KV_HEADS
WINDOW = 128
BLOCK = 128
SSM_WIDTH = 512
SSM_GROUP_CH = 16
SSM_GROUPS = 32
SSM_STATE = 64
D_FF = 2816
EPS = 1e-6
NEG = -1e30
LOG2E = math.log2(math.e)

ATTN_Q = N_HEADS * HEAD_DIM
ATTN_KV = N_KV_HEADS * HEAD_DIM
IN_COLS = 4096
COL_K = ATTN_Q
COL_V = ATTN_Q + ATTN_KV
COL_U = ATTN_Q + 2 * ATTN_KV
COL_GS = COL_U + SSM_WIDTH
COL_GA = COL_GS + D_MODEL

PAD = BLOCK - N_META
LP = PAD + N_META + SEQ
NB = LP // BLOCK

LANES = 128
SUBLANES = 8
VMEM_LIMIT = 56 * 1024 * 1024

TM = 688
MXU_TILE = 256
FF_CHUNK = 2 * MXU_TILE
TN_IN = 4 * MXU_TILE
T_FINAL = 1024

SEG = 172
T_SSM = SUBLANES * SEG
NT_SSM = LP // T_SSM
N_SLAB = SSM_WIDTH // LANES
G_SLAB = SSM_GROUPS // N_SLAB
SLAB_STATE = G_SLAB * SSM_STATE
SCAN_UNROLL = 4


def _cparams(sem):
    return pltpu.CompilerParams(dimension_semantics=sem, vmem_limit_bytes=VMEM_LIMIT)


def _rmsnorm(x, g):
    ms = jnp.mean(x * x, axis=-1, keepdims=True)
    return x * lax.rsqrt(ms + EPS) * g


def _ffn_kernel(x_ref, g_ref, wg_ref, wu_ref, wd_ref, o_ref):
    x = x_ref[...]
    xn = _rmsnorm(x, g_ref[...]).astype(bf16)
    acc = None
    for c0 in range(0, D_FF, FF_CHUNK):
        c1 = min(c0 + FF_CHUNK, D_FF)
        a = jnp.dot(xn, wg_ref[:, c0:c1], preferred_element_type=f32)
        b = jnp.dot(xn, wu_ref[:, c0:c1], preferred_element_type=f32)
        mid = (a * jax.nn.sigmoid(a) * b).astype(bf16)
        part = jnp.dot(mid, wd_ref[c0:c1, :], preferred_element_type=f32)
        acc = part if acc is None else acc + part
    o_ref[...] = x + 0.5 * acc


def _resident(shape, index_map):
    return pl.BlockSpec(shape, index_map, pipeline_mode=pl.Buffered(1))


def _ffn(l, h, g, wg, wu, wd):
    return pl.pallas_call(
        _ffn_kernel,
        grid=(LP // TM,),
        in_specs=[
            pl.BlockSpec((TM, D_MODEL), lambda i: (i, 0)),
            _resident((None, 1, D_MODEL), lambda i: (l, 0, 0)),
            _resident((None, D_MODEL, D_FF), lambda i: (l, 0, 0)),
            _resident((None, D_MODEL, D_FF), lambda i: (l, 0, 0)),
            _resident((None, D_FF, D_MODEL), lambda i: (l, 0, 0)),
        ],
        out_specs=pl.BlockSpec((TM, D_MODEL), lambda i: (i, 0)),
        out_shape=jax.ShapeDtypeStruct((LP, D_MODEL), f32),
        compiler_params=_cparams(("parallel",)),
        name="ffn",
    )(h, g, wg, wu, wd)


def _inproj_kernel(x_ref, g_ref, w_ref, o_ref, us_ref):
    xn = _rmsnorm(x_ref[...], g_ref[...]).astype(bf16)
    for c0 in range(0, IN_COLS, TN_IN):
        res = jnp.dot(xn, w_ref[:, c0:c0 + TN_IN], preferred_element_type=f32)
        o_ref[:, c0:c0 + TN_IN] = res.astype(bf16)
        for s in range(N_SLAB):
            u0 = COL_U + s * LANES
            if c0 <= u0 < c0 + TN_IN:
                us_ref[s] = res[:, u0 - c0:u0 - c0 + LANES]


def _inproj(l, h, g, w):
    return pl.pallas_call(
        _inproj_kernel,
        grid=(LP // TM,),
        in_specs=[
            pl.BlockSpec((TM, D_MODEL), lambda i: (i, 0)),
            _resident((None, 1, D_MODEL), lambda i: (l, 0, 0)),
            _resident((None, D_MODEL, IN_COLS), lambda i: (l, 0, 0)),
        ],
        out_specs=[pl.BlockSpec((TM, IN_COLS), lambda i: (i, 0)),
                   pl.BlockSpec((N_SLAB, TM, LANES), lambda i: (0, i, 0))],
        out_shape=[jax.ShapeDtypeStruct((LP, IN_COLS), bf16),
                   jax.ShapeDtypeStruct((N_SLAB, LP, LANES), f32)],
        compiler_params=_cparams(("parallel",)),
        name="inproj",
    )(h, g, w)


def _attn_kernel(sink_ref, q_ref, kp_ref, kc_ref, kn_ref, km_ref,
                 vp_ref, vc_ref, vn_ref, vm_ref, bias_ref, o_ref):
    k_all, v_all = [], []
    for kh in range(N_KV_HEADS):
        sl = slice(kh * HEAD_DIM, (kh + 1) * HEAD_DIM)
        k_all.append(jnp.concatenate([kp_ref[:, sl], kc_ref[:, sl], kn_ref[:, sl], km_ref[:, sl]], axis=0))
        v_all.append(jnp.concatenate([vp_ref[:, sl], vc_ref[:, sl], vn_ref[:, sl], vm_ref[:, sl]], axis=0))

    def logits(h):
        qh = q_ref[:, h * HEAD_DIM:(h + 1) * HEAD_DIM]
        return lax.dot_general(qh, k_all[h // Q_GROUP], (((1,), (1,)), ((), ())),
                               preferred_element_type=f32) + bias_ref[0, h]

    def softmax(h, s):
        sink = sink_ref[h] * LOG2E
        m = jnp.maximum(jnp.max(s, axis=-1, keepdims=True), sink)
        p = jnp.exp2(s - m)
        denom = jnp.sum(p, axis=-1, keepdims=True) + jnp.exp2(sink - m)
        return p.astype(bf16), denom

    outs = []
    s_q = {0: logits(0), 1: logits(1)}
    p_q = {0: softmax(0, s_q.pop(0))}
    for h in range(N_HEADS):
        if h + 2 < N_HEADS:
            s_q[h + 2] = logits(h + 2)
        if h + 1 < N_HEADS:
            p_q[h + 1] = softmax(h + 1, s_q.pop(h + 1))
        p, denom = p_q.pop(h)
        o = jnp.dot(p, v_all[h // Q_GROUP], preferred_element_type=f32)
        outs.append(o / denom)
    o_ref[...] = jnp.concatenate(outs, axis=1).astype(bf16)


def _bias_variant(n):
    return jnp.where(n == 0, 0, jnp.where(n == 1, 1, jnp.where(n == NB - 1, 3, 2)))


def _attn(proj, sink, bias):
    kv = lambda col, fn: pl.BlockSpec((BLOCK, ATTN_KV), lambda n: (fn(n), col))
    prev = lambda n: jnp.maximum(n - 1, 0)
    cur = lambda n: n
    nxt = lambda n: jnp.minimum(n + 1, NB - 1)
    first = lambda n: 0
    ck, cv = COL_K // ATTN_KV, COL_V // ATTN_KV
    return pl.pallas_call(
        _attn_kernel,
        grid=(NB,),
        in_specs=[
            pl.BlockSpec(memory_space=pltpu.SMEM),
            pl.BlockSpec((BLOCK, ATTN_Q), lambda n: (n, 0)),
            kv(ck, prev), kv(ck, cur), kv(ck, nxt), kv(ck, first),
            kv(cv, prev), kv(cv, cur), kv(cv, nxt), kv(cv, first),
            pl.BlockSpec((1, N_HEADS, BLOCK, 4 * BLOCK), lambda n: (_bias_variant(n), 0, 0, 0)),
        ],
        out_specs=pl.BlockSpec((BLOCK, ATTN_Q), lambda n: (n, 0)),
        out_shape=jax.ShapeDtypeStruct((LP, ATTN_Q), bf16),
        compiler_params=_cparams(("parallel",)),
        name="attn",
    )(sink, proj, proj, proj, proj, proj, proj, proj, proj, proj, bias)


def _attn_bias():
    qi = np.arange(BLOCK)[:, None]
    sj = np.arange(3 * BLOCK)[None, :]
    dist = np.abs(qi + BLOCK - sj)
    slopes = 2.0 ** (-8.0 * np.arange(1, N_HEADS + 1) / N_HEADS)
    band = -LOG2E * slopes[:, None, None] * dist[None].astype(np.float64)
    in_win = (dist <= WINDOW)[None]
    meta = np.where(np.arange(BLOCK) >= PAD, 0.0, NEG)[None, None, :]
    meta = np.broadcast_to(meta, (N_HEADS, BLOCK, BLOCK))
    out = []
    for blk_ok in ((False, False, True), (False, True, True), (True, True, True), (True, True, False)):
        kvalid = np.repeat(np.asarray(blk_ok), BLOCK)[None, None, :]
        b = np.where(in_win & kvalid, band, NEG)
        out.append(np.concatenate([b, meta], axis=-1))
    return jnp.asarray(np.stack(out), dtype=f32)


def _cmul(ar, ai, br, bi):
    return ar * br - ai * bi, ar * bi + ai * br


def _ssm_direction(reverse, u_ref, bm_ref, cm_ref, a_ref, as_ref, y_ref,
                   up_ref, bu_ref, xs_ref, yp_ref, carry_ref):
    d = 1 if reverse else 0
    row = lax.broadcasted_iota(jnp.int32, (SUBLANES, SLAB_STATE), 0)
    bcast = lambda v: jnp.broadcast_to(v, (SUBLANES, SLAB_STATE))

    def rows8(j):
        return pl.ds(pl.multiple_of(j * SUBLANES, SUBLANES), SUBLANES)

    def slab_body(s, _):
        def gather(j, _):
            up_ref[rows8(j), :] = u_ref[s, pl.ds(j, SUBLANES, stride=SEG), :]
            return 0

        lax.fori_loop(0, SEG, gather, 0, unroll=SCAN_UNROLL)
        bu_ref[...] = jnp.dot(up_ref[...].astype(bf16), bm_ref[d, s], preferred_element_type=f32)
        a = a_ref[d, s]
        are, aim = bcast(a[0]), bcast(a[1])

        def step(k, x):
            j = SEG - 1 - k if reverse else k
            blk = bu_ref[rows8(j), :]
            pr, pi = _cmul(are, aim, x[0], x[1])
            return pr + blk[:, :SLAB_STATE], pi + blk[:, SLAB_STATE:]

        zero = jnp.zeros((SUBLANES, SLAB_STATE), f32)
        er, ei = lax.fori_loop(0, SEG, step, (zero, zero), unroll=SCAN_UNROLL)

        asg = as_ref[d, s]
        asr, asi = bcast(asg[0]), bcast(asg[1])
        cr, ci = carry_ref[s, 0], carry_ref[s, 1]
        cin_r, cin_i = cr, ci
        order = range(SUBLANES - 1, -1, -1) if reverse else range(SUBLANES)
        for n, r in enumerate(order):
            if n > 0:
                cin_r = jnp.where(row == r, cr, cin_r)
                cin_i = jnp.where(row == r, ci, cin_i)
            pr, pi = _cmul(asr, asi, cr, ci)
            cr = pr + bcast(er[r:r + 1])
            ci = pi + bcast(ei[r:r + 1])
        carry_ref[s, 0] = cr
        carry_ref[s, 1] = ci

        def step2(k, x):
            xr, xi = step(k, x)
            j = SEG - 1 - k if reverse else k
            xs_ref[rows8(j), :SLAB_STATE] = xr
            xs_ref[rows8(j), SLAB_STATE:] = xi
            return xr, xi

        lax.fori_loop(0, SEG, step2, (cin_r, cin_i), unroll=SCAN_UNROLL)
        yp_ref[...] = jnp.dot(xs_ref[...].astype(bf16), cm_ref[d, s], preferred_element_type=f32)

        def scatter(j, _):
            y_ref[s, pl.ds(j, SUBLANES, stride=SEG), :] = yp_ref[rows8(j), :]
            return 0

        lax.fori_loop(0, SEG, scatter, 0, unroll=SCAN_UNROLL)
        return 0

    lax.fori_loop(0, N_SLAB, slab_body, 0)


def _ssm_kernel(uf_ref, ub_ref, bm_ref, cm_ref, a_ref, as_ref, yf_ref, yb_ref,
                up_ref, bu_ref, xs_ref, yp_ref, carry_ref):
    @pl.when(pl.program_id(0) == 0)
    def _():
        carry_ref[...] = jnp.zeros_like(carry_ref)

    scratch = (up_ref, bu_ref, xs_ref, yp_ref)
    _ssm_direction(False, uf_ref, bm_ref, cm_ref, a_ref, as_ref, yf_ref, *scratch, carry_ref.at[0])
    _ssm_direction(True, ub_ref, bm_ref, cm_ref, a_ref, as_ref, yb_ref, *scratch, carry_ref.at[1])


def _ssm(l, us, bm, cm, a, asg):
    fwd = pl.BlockSpec((N_SLAB, T_SSM, LANES), lambda i: (0, i, 0))
    bwd = pl.BlockSpec((N_SLAB, T_SSM, LANES), lambda i: (0, NT_SSM - 1 - i, 0))
    const = lambda *shape: _resident((None,) + shape, lambda i: (l,) + (0,) * len(shape))
    out = jax.ShapeDtypeStruct((N_SLAB, LP, LANES), f32)
    return pl.pallas_call(
        _ssm_kernel,
        grid=(NT_SSM,),
        in_specs=[
            fwd, bwd,
            const(2, N_SLAB, LANES, 2 * SLAB_STATE),
            const(2, N_SLAB, 2 * SLAB_STATE, LANES),
            const(2, N_SLAB, 2, 1, SLAB_STATE),
            const(2, N_SLAB, 2, 1, SLAB_STATE),
        ],
        out_specs=[fwd, bwd],
        out_shape=[out, out],
        scratch_shapes=[
            pltpu.VMEM((T_SSM, LANES), f32),
            pltpu.VMEM((T_SSM, 2 * SLAB_STATE), f32),
            pltpu.VMEM((T_SSM, 2 * SLAB_STATE), f32),
            pltpu.VMEM((T_SSM, LANES), f32),
            pltpu.VMEM((2, N_SLAB, 2, SUBLANES, SLAB_STATE), f32),
        ],
        compiler_params=_cparams(("arbitrary",)),
        name="ssm",
    )(us, us, bm, cm, a, asg)


def _ssm_params(lam_re, lam_im, log_dt, b_re, b_im, c_re, c_im):
    dt = jnp.exp(log_dt)[..., None]
    mag = jnp.exp(lam_re * dt)
    ar, ai = mag * jnp.cos(lam_im * dt), mag * jnp.sin(lam_im * dt)
    mag_s = jnp.exp(lam_re * dt * SEG)
    sr, si = mag_s * jnp.cos(lam_im * dt * SEG), mag_s * jnp.sin(lam_im * dt * SEG)
    den = lam_re * lam_re + lam_im * lam_im
    fr = ((ar - 1.0) * lam_re + ai * lam_im) / den
    fi = (ai * lam_re - (ar - 1.0) * lam_im) / den
    bbr = fr[..., None] * b_re - fi[..., None] * b_im
    bbi = fr[..., None] * b_im + fi[..., None] * b_re
    eye = jnp.eye(G_SLAB, dtype=f32)

    def bmat(b):
        b = b.reshape(DEPTH, 2, N_SLAB, G_SLAB, SSM_STATE, SSM_GROUP_CH)
        m = jnp.einsum('ldsgpi,gh->ldsgihp', b, eye)
        return m.reshape(DEPTH, 2, N_SLAB, LANES, SLAB_STATE)

    def cmat(c):
        c = c.reshape(DEPTH, 2, N_SLAB, G_SLAB, SSM_GROUP_CH, SSM_STATE)
        m = jnp.einsum('ldsgop,gh->ldshpgo', c, eye)
        return m.reshape(DEPTH, 2, N_SLAB, SLAB_STATE, LANES)

    bm = jnp.concatenate([bmat(bbr), bmat(bbi)], axis=-1).astype(bf16)
    cm = jnp.concatenate([cmat(c_re), -cmat(c_im)], axis=-2).astype(bf16)

    def vec(r, i):
        v = jnp.stack([r, i], axis=2)
        v = v.reshape(DEPTH, 2, 2, N_SLAB, 1, SLAB_STATE)
        return v.transpose(0, 1, 3, 2, 4, 5)

    return bm, cm, vec(ar, ai), vec(sr, si)


def _gelu_tanh(x):
    c = math.sqrt(2.0 / math.pi)
    return 0.5 * x * (1.0 + jnp.tanh(c * (x + 0.044715 * (x * x * x))))


def _merge_kernel(h_ref, yf_ref, yb_ref, us_ref, at_ref, gs_ref, ga_ref, d_ref,
                  wglu_ref, wbs_ref, wba_ref, wout_ref, o_ref):
    lanes = lambda ref: jnp.concatenate([ref[s] for s in range(N_SLAB)], axis=1)
    y = lanes(yf_ref) + lanes(yb_ref) + d_ref[...] * lanes(us_ref)
    z = _gelu_tanh(y)
    gl = jnp.dot(z.astype(bf16), wglu_ref[...], preferred_element_type=f32)
    ys = (z * jax.nn.sigmoid(gl)).astype(bf16)
    m1 = jnp.dot(ys, wbs_ref[...], preferred_element_type=f32)
    m2 = jnp.dot(at_ref[...], wba_ref[...], preferred_element_type=f32)
    merged = (jax.nn.sigmoid(gs_ref[...].astype(f32)) * m1
              + jax.nn.sigmoid(ga_ref[...].astype(f32)) * m2)
    out = h_ref[...] + jnp.dot(merged.astype(bf16), wout_ref[...], preferred_element_type=f32)
    rows = pl.program_id(0) * TM + lax.broadcasted_iota(jnp.int32, (TM, 1), 0)
    o_ref[...] = jnp.where(rows >= PAD, out, 0.0)


def _merge(l, h, yf, yb, us, proj, attn, d, wglu, wbs, wba, wout):
    row = lambda w, col: pl.BlockSpec((TM, w), lambda i: (i, col))
    slab = pl.BlockSpec((N_SLAB, TM, LANES), lambda i: (0, i, 0))
    full = lambda r, c: _resident((None, r, c), lambda i: (l, 0, 0))
    return pl.pallas_call(
        _merge_kernel,
        grid=(LP // TM,),
        in_specs=[
            row(D_MODEL, 0), slab, slab, slab, row(ATTN_Q, 0),
            row(D_MODEL, COL_GS // D_MODEL), row(D_MODEL, COL_GA // D_MODEL),
            full(1, SSM_WIDTH), full(SSM_WIDTH, SSM_WIDTH), full(SSM_WIDTH, D_MODEL),
            full(ATTN_Q, D_MODEL), full(D_MODEL, D_MODEL),
        ],
        out_specs=row(D_MODEL, 0),
        out_shape=jax.ShapeDtypeStruct((LP, D_MODEL), f32),
        compiler_params=_cparams(("parallel",)),
        name="merge",
    )(h, yf, yb, us, attn, proj, proj, d, wglu, wbs, wba, wout)


def _final_kernel(x_ref, g_ref, o_ref):
    o_ref[...] = _rmsnorm(x_ref[...], g_ref[...])


def _final_norm(h, g):
    return pl.pallas_call(
        _final_kernel,
        grid=(SEQ // T_FINAL,),
        in_specs=[pl.BlockSpec((pl.Element(T_FINAL), pl.Element(D_MODEL)),
                               lambda i: (pl.multiple_of((i * (T_FINAL // BLOCK) + 1) * BLOCK, BLOCK), 0)),
                  pl.BlockSpec((1, D_MODEL), lambda i: (0, 0))],
        out_specs=pl.BlockSpec((T_FINAL, D_MODEL), lambda i: (i, 0)),
        out_shape=jax.ShapeDtypeStruct((SEQ, D_MODEL), f32),
        compiler_params=_cparams(("parallel",)),
        name="final_norm",
    )(h, g)


def kernel(x, meta_tokens, ffn1_norm, ffn1_w_gate, ffn1_w_up, ffn1_w_down, mix_norm, w_in, ssm_lam_re, ssm_lam_im, ssm_log_dt, ssm_b_re, ssm_b_im, ssm_c_re, ssm_c_im, ssm_d, ssm_w_glu, attn_sink, w_branch_ssm, w_branch_attn, w_out, ffn2_norm, ffn2_w_gate, ffn2_w_up, ffn2_w_down, final_norm):
    assert x.shape == (1, SEQ, D_MODEL)
    h = jnp.concatenate([jnp.zeros((PAD, D_MODEL), f32), meta_tokens.astype(f32), x[0]], axis=0)

    col_scale = jnp.where(jnp.arange(IN_COLS) < ATTN_Q, LOG2E * HEAD_DIM ** -0.5, 1.0).astype(f32)
    w_in_b = (w_in * col_scale).astype(bf16)
    cast = lambda w: w.astype(bf16)
    f1g, f1u, f1d = cast(ffn1_w_gate), cast(ffn1_w_up), cast(ffn1_w_down)
    f2g, f2u, f2d = cast(ffn2_w_gate), cast(ffn2_w_up), cast(ffn2_w_down)
    wglu, wbs, wba, wo = cast(ssm_w_glu), cast(w_branch_ssm), cast(w_branch_attn), cast(w_out)
    bm, cm, a_vec, as_vec = _ssm_params(ssm_lam_re, ssm_lam_im, ssm_log_dt,
                                        ssm_b_re, ssm_b_im, ssm_c_re, ssm_c_im)
    bias = _attn_bias()

    n1, nm, n2, dskip = (v[:, None, :] for v in (ffn1_norm, mix_norm, ffn2_norm, ssm_d))
    for l in range(DEPTH):
        h = _ffn(l, h, n1, f1g, f1u, f1d)
        proj, us = _inproj(l, h, nm, w_in_b)
        attn = _attn(proj, attn_sink[l], bias)
        yf, yb = _ssm(l, us, bm, cm, a_vec, as_vec)
        h = _merge(l, h, yf, yb, us, proj, attn, dskip, wglu, wbs, wba, wo)
        h = _ffn(l, h, n2, f2g, f2u, f2d)
    return _final_norm(h, final_norm[None])[None]
```

```python
import functools
import math

import numpy as np
import jax
import jax.numpy as jnp
from jax import lax
from jax.experimental import pallas as pl
from jax.experimental.pallas import tpu as pltpu

f32 = jnp.float32
bf16 = jnp.bfloat16

D_MODEL = 1024
SEQ = 16384
DEPTH = 4
N_META = 16
N_HEADS = 16
N_KV_HEADS = 4
HEAD_DIM = 64
Q_GROUP = N_HEADS // N_KV_HEADS
WINDOW = 128
BLOCK = 128
SSM_WIDTH = 512
SSM_GROUP_CH = 16
SSM_GROUPS = 32
SSM_STATE = 64
D_FF = 2816
EPS = 1e-6
NEG = -1e30
LOG2E = math.log2(math.e)

ATTN_Q = N_HEADS * HEAD_DIM
ATTN_KV = N_KV_HEADS * HEAD_DIM
IN_COLS = 4096
COL_K = ATTN_Q
COL_V = ATTN_Q + ATTN_KV
COL_U = ATTN_Q + 2 * ATTN_KV
COL_GS = COL_U + SSM_WIDTH
COL_GA = COL_GS + D_MODEL

PAD = BLOCK - N_META
LP = PAD + N_META + SEQ
NB = LP // BLOCK

LANES = 128
SUBLANES = 8
VMEM_LIMIT = 56 * 1024 * 1024

TM = 688
MXU_TILE = 256
FF_CHUNK = 2 * MXU_TILE
TN_IN = 4 * MXU_TILE
T_FINAL = 1024

CH_T = SUBLANES
N_SLAB = SSM_WIDTH // LANES
G_SLAB = SSM_GROUPS // N_SLAB
SLAB_STATE = G_SLAB * SSM_STATE
CAT = CH_T * LANES
T_SSM = LP // 3
NT_SSM = LP // T_SSM
R_SSM = T_SSM // CH_T
R_VREG = R_SSM // SUBLANES


def _cparams(sem):
    return pltpu.CompilerParams(dimension_semantics=sem, vmem_limit_bytes=VMEM_LIMIT)


def _rmsnorm(x, g):
    ms = jnp.mean(x * x, axis=-1, keepdims=True)
    return x * lax.rsqrt(ms + EPS) * g


def _ffn_kernel(x_ref, g_ref, wg_ref, wu_ref, wd_ref, o_ref):
    x = x_ref[...]
    xn = _rmsnorm(x, g_ref[...]).astype(bf16)
    acc = None
    for c0 in range(0, D_FF, FF_CHUNK):
        c1 = min(c0 + FF_CHUNK, D_FF)
        a = jnp.dot(xn, wg_ref[:, c0:c1], preferred_element_type=f32)
        b = jnp.dot(xn, wu_ref[:, c0:c1], preferred_element_type=f32)
        mid = (a * jax.nn.sigmoid(a) * b).astype(bf16)
        part = jnp.dot(mid, wd_ref[c0:c1, :], preferred_element_type=f32)
        acc = part if acc is None else acc + part
    o_ref[...] = x + 0.5 * acc


def _resident(shape, index_map):
    return pl.BlockSpec(shape, index_map, pipeline_mode=pl.Buffered(1))


def _ffn(l, h, g, wg, wu, wd):
    return pl.pallas_call(
        _ffn_kernel,
        grid=(LP // TM,),
        in_specs=[
            pl.BlockSpec((TM, D_MODEL), lambda i: (i, 0)),
            _resident((None, 1, D_MODEL), lambda i: (l, 0, 0)),
            _resident((None, D_MODEL, D_FF), lambda i: (l, 0, 0)),
            _resident((None, D_MODEL, D_FF), lambda i: (l, 0, 0)),
            _resident((None, D_FF, D_MODEL), lambda i: (l, 0, 0)),
        ],
        out_specs=pl.BlockSpec((TM, D_MODEL), lambda i: (i, 0)),
        out_shape=jax.ShapeDtypeStruct((LP, D_MODEL), f32),
        compiler_params=_cparams(("parallel",)),
        name="ffn",
    )(h, g, wg, wu, wd)


def _inproj_kernel(x_ref, g_ref, w_ref, o_ref, us_ref):
    xn = _rmsnorm(x_ref[...], g_ref[...]).astype(bf16)
    for c0 in range(0, IN_COLS, TN_IN):
        res = jnp.dot(xn, w_ref[:, c0:c0 + TN_IN], preferred_element_type=f32)
        o_ref[:, c0:c0 + TN_IN] = res.astype(bf16)
        for s in range(N_SLAB):
            u0 = COL_U + s * LANES
            if c0 <= u0 < c0 + TN_IN:
                us_ref[s] = res[:, u0 - c0:u0 - c0 + LANES]


def _inproj(l, h, g, w):
    return pl.pallas_call(
        _inproj_kernel,
        grid=(LP // TM,),
        in_specs=[
            pl.BlockSpec((TM, D_MODEL), lambda i: (i, 0)),
            _resident((None, 1, D_MODEL), lambda i: (l, 0, 0)),
            _resident((None, D_MODEL, IN_COLS), lambda i: (l, 0, 0)),
        ],
        out_specs=[pl.BlockSpec((TM, IN_COLS), lambda i: (i, 0)),
                   pl.BlockSpec((N_SLAB, TM, LANES), lambda i: (0, i, 0))],
        out_shape=[jax.ShapeDtypeStruct((LP, IN_COLS), bf16),
                   jax.ShapeDtypeStruct((N_SLAB, LP, LANES), f32)],
        compiler_params=_cparams(("parallel",)),
        name="inproj",
    )(h, g, w)


def _attn_kernel(sink_ref, q_ref, kp_ref, kc_ref, kn_ref, km_ref,
                 vp_ref, vc_ref, vn_ref, vm_ref, bias_ref, o_ref):
    k_all, v_all = [], []
    for kh in range(N_KV_HEADS):
        sl = slice(kh * HEAD_DIM, (kh + 1) * HEAD_DIM)
        k_all.append(jnp.concatenate([kp_ref[:, sl], kc_ref[:, sl], kn_ref[:, sl], km_ref[:, sl]], axis=0))
        v_all.append(jnp.concatenate([vp_ref[:, sl], vc_ref[:, sl], vn_ref[:, sl], vm_ref[:, sl]], axis=0))

    def logits(h):
        qh = q_ref[:, h * HEAD_DIM:(h + 1) * HEAD_DIM]
        return lax.dot_general(qh, k_all[h // Q_GROUP], (((1,), (1,)), ((), ())),
                               preferred_element_type=f32) + bias_ref[0, h]

    def softmax(h, s):
        sink = sink_ref[h] * LOG2E
        m = jnp.maximum(jnp.max(s, axis=-1, keepdims=True), sink)
        p = jnp.exp2(s - m)
        denom = jnp.sum(p, axis=-1, keepdims=True) + jnp.exp2(sink - m)
        return p.astype(bf16), denom

    outs = []
    s_q = {0: logits(0), 1: logits(1)}
    p_q = {0: softmax(0, s_q.pop(0))}
    for h in range(N_HEADS):
        if h + 2 < N_HEADS:
            s_q[h + 2] = logits(h + 2)
        if h + 1 < N_HEADS:
            p_q[h + 1] = softmax(h + 1, s_q.pop(h + 1))
        p, denom = p_q.pop(h)
        o = jnp.dot(p, v_all[h // Q_GROUP], preferred_element_type=f32)
        outs.append(o / denom)
    o_ref[...] = jnp.concatenate(outs, axis=1).astype(bf16)


def _bias_variant(n):
    return jnp.where(n == 0, 0, jnp.where(n == 1, 1, jnp.where(n == NB - 1, 3, 2)))


def _attn(proj, sink, bias):
    kv = lambda col, fn: pl.BlockSpec((BLOCK, ATTN_KV), lambda n: (fn(n), col))
    prev = lambda n: jnp.maximum(n - 1, 0)
    cur = lambda n: n
    nxt = lambda n: jnp.minimum(n + 1, NB - 1)
    first = lambda n: 0
    ck, cv = COL_K // ATTN_KV, COL_V // ATTN_KV
    return pl.pallas_call(
        _attn_kernel,
        grid=(NB,),
        in_specs=[
            pl.BlockSpec(memory_space=pltpu.SMEM),
            pl.BlockSpec((BLOCK, ATTN_Q), lambda n: (n, 0)),
            kv(ck, prev), kv(ck, cur), kv(ck, nxt), kv(ck, first),
            kv(cv, prev), kv(cv, cur), kv(cv, nxt), kv(cv, first),
            pl.BlockSpec((1, N_HEADS, BLOCK, 4 * BLOCK), lambda n: (_bias_variant(n), 0, 0, 0)),
        ],
        out_specs=pl.BlockSpec((BLOCK, ATTN_Q), lambda n: (n, 0)),
        out_shape=jax.ShapeDtypeStruct((LP, ATTN_Q), bf16),
        compiler_params=_cparams(("parallel",)),
        name="attn",
    )(sink, proj, proj, proj, proj, proj, proj, proj, proj, proj, bias)


def _attn_bias():
    qi = np.arange(BLOCK)[:, None]
    sj = np.arange(3 * BLOCK)[None, :]
    dist = np.abs(qi + BLOCK - sj)
    slopes = 2.0 ** (-8.0 * np.arange(1, N_HEADS + 1) / N_HEADS)
    band = -LOG2E * slopes[:, None, None] * dist[None].astype(np.float64)
    in_win = (dist <= WINDOW)[None]
    meta = np.where(np.arange(BLOCK) >= PAD, 0.0, NEG)[None, None, :]
    meta = np.broadcast_to(meta, (N_HEADS, BLOCK, BLOCK))
    out = []
    for blk_ok in ((False, False, True), (False, True, True), (True, True, True), (True, True, False)):
        kvalid = np.repeat(np.asarray(blk_ok), BLOCK)[None, None, :]
        b = np.where(in_win & kvalid, band, NEG)
        out.append(np.concatenate([b, meta], axis=-1))
    return jnp.asarray(np.stack(out), dtype=f32)


def _cmul(ar, ai, br, bi):
    return ar * br - ai * bi, ar * bi + ai * br


def _ssm_kernel(reverse, u_ref, w_ref, m_ref, v_ref, ast_ref, pc_ref, y_ref,
                ucat_ref, loc_ref, sp_ref, ycat_ref, carry_ref):
    @pl.when(pl.program_id(1) == 0)
    def _():
        carry_ref[...] = jnp.zeros_like(carry_ref)

    def rows8(k):
        return pl.ds(pl.multiple_of(k * SUBLANES, SUBLANES), SUBLANES)

    def token_rows(k, t):
        return pl.ds(k * (SUBLANES * CH_T) + t, SUBLANES, stride=CH_T)

    def gather(k, _):
        for t in range(CH_T):
            ucat_ref[rows8(k), t * LANES:(t + 1) * LANES] = u_ref[token_rows(k, t), :]
        return 0

    lax.fori_loop(0, R_VREG, gather, 0, unroll=2)
    ucat = ucat_ref[...].astype(bf16)
    loc_ref[...] = jnp.dot(ucat, w_ref[...], preferred_element_type=f32)
    ycat_ref[...] = jnp.dot(ucat, m_ref[...], preferred_element_type=f32)

    row = lax.broadcasted_iota(jnp.int32, (SUBLANES, SLAB_STATE), 0)
    edge = SUBLANES - 1 if reverse else 0
    last = 0 if reverse else SUBLANES - 1
    toward = lambda x, n: pltpu.roll(x, (SUBLANES - n) if reverse else n, 0)

    def scan(n, carry):
        k = R_VREG - 1 - n if reverse else n
        xr = loc_ref[rows8(k), :SLAB_STATE]
        xi = loc_ref[rows8(k), SLAB_STATE:]
        for j in range(3):
            tr, ti = _cmul(ast_ref[j, 0], ast_ref[j, 1], toward(xr, 1 << j), toward(xi, 1 << j))
            xr, xi = xr + tr, xi + ti
        cr, ci = carry
        tr, ti = _cmul(pc_ref[0], pc_ref[1], cr, ci)
        xr, xi = xr + tr, xi + ti
        sp_ref[rows8(k), :SLAB_STATE] = jnp.where(row == edge, cr, toward(xr, 1))
        sp_ref[rows8(k), SLAB_STATE:] = jnp.where(row == edge, ci, toward(xi, 1))
        bcast = lambda x: jnp.broadcast_to(x[last:last + 1], (SUBLANES, SLAB_STATE))
        return bcast(xr), bcast(xi)

    cr, ci = lax.fori_loop(0, R_VREG, scan, (carry_ref[0], carry_ref[1]))
    carry_ref[0] = cr
    carry_ref[1] = ci
    ycat_ref[...] += jnp.dot(sp_ref[...].astype(bf16), v_ref[...], preferred_element_type=f32)

    def scatter(k, _):
        for t in range(CH_T):
            y_ref[token_rows(k, t), :] = ycat_ref[rows8(k), t * LANES:(t + 1) * LANES]
        return 0

    lax.fori_loop(0, R_VREG, scatter, 0, unroll=2)


def _ssm(l, d, us, w, m, v, ast, pc):
    tile = (lambda i: NT_SSM - 1 - i) if d else (lambda i: i)
    rows = pl.BlockSpec((None, T_SSM, LANES), lambda s, i: (s, tile(i), 0))
    mat = pl.BlockSpec((None, None, None, CAT, CAT), lambda s, i: (l, d, s, 0, 0))
    return pl.pallas_call(
        functools.partial(_ssm_kernel, bool(d)),
        grid=(N_SLAB, NT_SSM),
        in_specs=[
            rows, mat, mat, mat,
            pl.BlockSpec((None, None, None, 3, 2, SUBLANES, SLAB_STATE), lambda s, i: (l, d, s, 0, 0, 0, 0)),
            pl.BlockSpec((None, None, None, 2, SUBLANES, SLAB_STATE), lambda s, i: (l, d, s, 0, 0, 0)),
        ],
        out_specs=rows,
        out_shape=jax.ShapeDtypeStruct((N_SLAB, LP, LANES), f32),
        scratch_shapes=[
            pltpu.VMEM((R_SSM, CAT), f32),
            pltpu.VMEM((R_SSM, CAT), f32),
            pltpu.VMEM((R_SSM, CAT), f32),
            pltpu.VMEM((R_SSM, CAT), f32),
            pltpu.VMEM((2, SUBLANES, SLAB_STATE), f32),
        ],
        compiler_params=_cparams(("parallel", "arbitrary")),
        name="ssm",
    )(us, w, m, v, ast, pc)


def _ssm_params(lam_re, lam_im, log_dt, b_re, b_im, c_re, c_im):
    hi = lax.Precision.HIGHEST
    dt = jnp.exp(log_dt)[..., None]

    def apow(n):
        mag = jnp.exp(lam_re * dt * n)
        return mag * jnp.cos(lam_im * dt * n), mag * jnp.sin(lam_im * dt * n)

    ar, ai = apow(1)
    den = lam_re * lam_re + lam_im * lam_im
    fr = ((ar - 1.0) * lam_re + ai * lam_im) / den
    fi = (ai * lam_re - (ar - 1.0) * lam_im) / den
    bbr = fr[..., None] * b_re - fi[..., None] * b_im
    bbi = fr[..., None] * b_im + fi[..., None] * b_re

    pows = [apow(n) for n in range(CH_T + 1)]
    abr = jnp.stack([p[0][..., None] * bbr - p[1][..., None] * bbi for p in pows[:CH_T]])
    abi = jnp.stack([p[0][..., None] * bbi + p[1][..., None] * bbr for p in pows[:CH_T]])
    car = jnp.stack([c_re * p[0][..., None, :] - c_im * p[1][..., None, :] for p in pows[1:]])
    cai = jnp.stack([c_re * p[1][..., None, :] + c_im * p[0][..., None, :] for p in pows[1:]])
    kk = (jnp.einsum('ldgop,nldgpi->nldgoi', c_re, abr, precision=hi)
          - jnp.einsum('ldgop,nldgpi->nldgoi', c_im, abi, precision=hi))

    def by_dir(x):
        return jnp.stack([x[:, :, 0], x[::-1][:, :, 1]], axis=2)

    eye = jnp.eye(G_SLAB, dtype=f32)
    slab = lambda x: x.reshape(x.shape[:3] + (N_SLAB, G_SLAB) + x.shape[4:])

    def wmat(ab):
        x = slab(by_dir(ab[::-1]))
        x = x.transpose(1, 2, 3, 0, 4, 6, 5)
        x = x[..., None, :] * eye[:, None, :, None]
        return x.reshape(DEPTH, 2, N_SLAB, CAT, SLAB_STATE)

    w = jnp.concatenate([wmat(abr), wmat(abi)], axis=-1).astype(bf16)

    def vmat(ca):
        x = slab(by_dir(ca))
        x = x.transpose(1, 2, 3, 4, 6, 0, 5)
        x = x[..., None, :] * eye[:, None, None, :, None]
        return x.reshape(DEPTH, 2, N_SLAB, SLAB_STATE, CAT)

    v = jnp.concatenate([vmat(car), -vmat(cai)], axis=-2).astype(bf16)

    s_idx, t_idx = np.arange(CH_T)[:, None], np.arange(CH_T)[None, :]
    causal = np.stack([s_idx <= t_idx, s_idx >= t_idx], axis=-1).astype(np.float32)
    x = kk[np.abs(t_idx - s_idx)] * causal[:, :, None, :, None, None, None]
    x = x.reshape(CH_T, CH_T, DEPTH, 2, N_SLAB, G_SLAB, SSM_GROUP_CH, SSM_GROUP_CH)
    x = x.transpose(2, 3, 4, 0, 5, 7, 1, 6)
    x = x[..., None, :] * eye[:, None, None, :, None]
    m = x.reshape(DEPTH, 2, N_SLAB, CAT, CAT).astype(bf16)

    def lanes(n):
        r, i = apow(n)
        return jnp.stack([r, i], axis=2).reshape(DEPTH, 2, 2, N_SLAB, 1, SLAB_STATE)

    r8 = np.arange(SUBLANES)
    steps = []
    for j in range(3):
        sh = 1 << j
        has_partner = np.stack([r8 >= sh, r8 < SUBLANES - sh]).astype(np.float32)
        steps.append(lanes(CH_T * sh) * has_partner[None, :, None, None, :, None])
    ast = jnp.stack(steps, axis=2)
    ast = ast.transpose(0, 1, 4, 2, 3, 5, 6)
    per_row = jnp.concatenate([lanes(CH_T * (r + 1)) for r in range(SUBLANES)], axis=4)
    pc = jnp.stack([per_row[:, 0], per_row[:, 1, :, :, ::-1]], axis=1)
    pc = pc.transpose(0, 1, 3, 2, 4, 5)
    return w, m, v, ast, pc


def _gelu_tanh(x):
    c = math.sqrt(2.0 / math.pi)
    return 0.5 * x * (1.0 + jnp.tanh(c * (x + 0.044715 * (x * x * x))))


def _merge_kernel(h_ref, yf_ref, yb_ref, us_ref, at_ref, gs_ref, ga_ref, d_ref,
                  wglu_ref, wbs_ref, wba_ref, wout_ref, o_ref):
    lanes = lambda ref: jnp.concatenate([ref[s] for s in range(N_SLAB)], axis=1)
    y = lanes(yf_ref) + lanes(yb_ref) + d_ref[...] * lanes(us_ref)
    z = _gelu_tanh(y)
    gl = jnp.dot(z.astype(bf16), wglu_ref[...], preferred_element_type=f32)
    ys = (z * jax.nn.sigmoid(gl)).astype(bf16)
    m1 = jnp.dot(ys, wbs_ref[...], preferred_element_type=f32)
    m2 = jnp.dot(at_ref[...], wba_ref[...], preferred_element_type=f32)
    merged = (jax.nn.sigmoid(gs_ref[...].astype(f32)) * m1
              + jax.nn.sigmoid(ga_ref[...].astype(f32)) * m2)
    out = h_ref[...] + jnp.dot(merged.astype(bf16), wout_ref[...], preferred_element_type=f32)
    rows = pl.program_id(0) * TM + lax.broadcasted_iota(jnp.int32, (TM, 1), 0)
    o_ref[...] = jnp.where(rows >= PAD, out, 0.0)


def _merge(l, h, yf, yb, us, proj, attn, d, wglu, wbs, wba, wout):
    row = lambda w, col: pl.BlockSpec((TM, w), lambda i: (i, col))
    slab = pl.BlockSpec((N_SLAB, TM, LANES), lambda i: (0, i, 0))
    full = lambda r, c: _resident((None, r, c), lambda i: (l, 0, 0))
    return pl.pallas_call(
        _merge_kernel,
        grid=(LP // TM,),
        in_specs=[
            row(D_MODEL, 0), slab, slab, slab, row(ATTN_Q, 0),
            row(D_MODEL, COL_GS // D_MODEL), row(D_MODEL, COL_GA // D_MODEL),
            full(1, SSM_WIDTH), full(SSM_WIDTH, SSM_WIDTH), full(SSM_WIDTH, D_MODEL),
            full(ATTN_Q, D_MODEL), full(D_MODEL, D_MODEL),
        ],
        out_specs=row(D_MODEL, 0),
        out_shape=jax.ShapeDtypeStruct((LP, D_MODEL), f32),
        compiler_params=_cparams(("parallel",)),
        name="merge",
    )(h, yf, yb, us, attn, proj, proj, d, wglu, wbs, wba, wout)


def _final_kernel(x_ref, g_ref, o_ref):
    o_ref[...] = _rmsnorm(x_ref[...], g_ref[...])


def _final_norm(h, g):
    return pl.pallas_call(
        _final_kernel,
        grid=(SEQ // T_FINAL,),
        in_specs=[pl.BlockSpec((pl.Element(T_FINAL), pl.Element(D_MODEL)),
                               lambda i: (pl.multiple_of((i * (T_FINAL // BLOCK) + 1) * BLOCK, BLOCK), 0)),
                  pl.BlockSpec((1, D_MODEL), lambda i: (0, 0))],
        out_specs=pl.BlockSpec((T_FINAL, D_MODEL), lambda i: (i, 0)),
        out_shape=jax.ShapeDtypeStruct((SEQ, D_MODEL), f32),
        compiler_params=_cparams(("parallel",)),
        name="final_norm",
    )(h, g)


def kernel(x, meta_tokens, ffn1_norm, ffn1_w_gate, ffn1_w_up, ffn1_w_down, mix_norm, w_in, ssm_lam_re, ssm_lam_im, ssm_log_dt, ssm_b_re, ssm_b_im, ssm_c_re, ssm_c_im, ssm_d, ssm_w_glu, attn_sink, w_branch_ssm, w_branch_attn, w_out, ffn2_norm, ffn2_w_gate, ffn2_w_up, ffn2_w_down, final_norm):
    assert x.shape == (1, SEQ, D_MODEL)
    h = jnp.concatenate([jnp.zeros((PAD, D_MODEL), f32), meta_tokens.astype(f32), x[0]], axis=0)

    col_scale = jnp.where(jnp.arange(IN_COLS) < ATTN_Q, LOG2E * HEAD_DIM ** -0.5, 1.0).astype(f32)
    w_in_b = (w_in * col_scale).astype(bf16)
    cast = lambda w: w.astype(bf16)
    f1g, f1u, f1d = cast(ffn1_w_gate), cast(ffn1_w_up), cast(ffn1_w_down)
    f2g, f2u, f2d = cast(ffn2_w_gate), cast(ffn2_w_up), cast(ffn2_w_down)
    wglu, wbs, wba, wo = cast(ssm_w_glu), cast(w_branch_ssm), cast(w_branch_attn), cast(w_out)
    ssm_mats = _ssm_params(ssm_lam_re, ssm_lam_im, ssm_log_dt, ssm_b_re, ssm_b_im, ssm_c_re, ssm_c_im)
    bias = _attn_bias()

    n1, nm, n2, dskip = (v[:, None, :] for v in (ffn1_norm, mix_norm, ffn2_norm, ssm_d))
    for l in range(DEPTH):
        h = _ffn(l, h, n1, f1g, f1u, f1d)
        proj, us = _inproj(l, h, nm, w_in_b)
        attn = _attn(proj, attn_sink[l], bias)
        yf = _ssm(l, 0, us, *ssm_mats)
        yb = _ssm(l, 1, us, *ssm_mats)
        h = _merge(l, h, yf, yb, us, proj, attn, dskip, wglu, wbs, wba, wo)
        h = _ffn(l, h, n2, f2g, f2u, f2d)
    return _final_norm(h, final_norm[None])[None]
```

```python
import functools
import math

import numpy as np
import jax
import jax.numpy as jnp
from jax import lax
from jax.experimental import pallas as pl
from jax.experimental.pallas import tpu as pltpu

f32 = jnp.float32
bf16 = jnp.bfloat16

D_MODEL = 1024
SEQ = 16384
DEPTH = 4
N_META = 16
N_HEADS = 16
N_KV_HEADS = 4
HEAD_DIM = 64
Q_GROUP = N_HEADS // N_KV_HEADS
WINDOW = 128
BLOCK = 128
SSM_WIDTH = 512
SSM_GROUP_CH = 16
SSM_GROUPS = 32
SSM_STATE = 64
D_FF = 2816
EPS = 1e-6
NEG = -1e30
LOG2E = math.log2(math.e)

ATTN_Q = N_HEADS * HEAD_DIM
ATTN_KV = N_KV_HEADS * HEAD_DIM
IN_COLS = 4096
COL_K = ATTN_Q
COL_V = ATTN_Q + ATTN_KV
COL_U = ATTN_Q + 2 * ATTN_KV
COL_GS = COL_U + SSM_WIDTH
COL_GA = COL_GS + D_MODEL

PAD = BLOCK - N_META
LP = PAD + N_META + SEQ
NB = LP // BLOCK

LANES = 128
SUBLANES = 8
VMEM_LIMIT = 56 * 1024 * 1024

TM = 688
MXU_TILE = 256
FF_CHUNK = 2 * MXU_TILE
TN_IN = 4 * MXU_TILE
T_FINAL = 1024

CH_T = SUBLANES
N_SLAB = SSM_WIDTH // LANES
G_SLAB = SSM_GROUPS // N_SLAB
SLAB_STATE = G_SLAB * SSM_STATE
CAT = CH_T * LANES
T_SSM = LP // 3
NT_SSM = LP // T_SSM
R_SSM = T_SSM // CH_T
R_VREG = R_SSM // SUBLANES
EXPAND_ROWS = 256


def _cparams(sem):
    return pltpu.CompilerParams(dimension_semantics=sem, vmem_limit_bytes=VMEM_LIMIT)


def _rmsnorm(x, g):
    ms = jnp.mean(x * x, axis=-1, keepdims=True)
    return x * lax.rsqrt(ms + EPS) * g


def _ffn_kernel(x_ref, g_ref, wg_ref, wu_ref, wd_ref, o_ref):
    x = x_ref[...]
    xn = _rmsnorm(x, g_ref[...]).astype(bf16)
    acc = None
    for c0 in range(0, D_FF, FF_CHUNK):
        c1 = min(c0 + FF_CHUNK, D_FF)
        a = jnp.dot(xn, wg_ref[:, c0:c1], preferred_element_type=f32)
        b = jnp.dot(xn, wu_ref[:, c0:c1], preferred_element_type=f32)
        mid = (a * jax.nn.sigmoid(a) * b).astype(bf16)
        part = jnp.dot(mid, wd_ref[c0:c1, :], preferred_element_type=f32)
        acc = part if acc is None else acc + part
    o_ref[...] = x + 0.5 * acc


def _resident(shape, index_map):
    return pl.BlockSpec(shape, index_map, pipeline_mode=pl.Buffered(1))


def _ffn(l, h, g, wg, wu, wd):
    return pl.pallas_call(
        _ffn_kernel,
        grid=(LP // TM,),
        in_specs=[
            pl.BlockSpec((TM, D_MODEL), lambda i: (i, 0)),
            _resident((None, 1, D_MODEL), lambda i: (l, 0, 0)),
            _resident((None, D_MODEL, D_FF), lambda i: (l, 0, 0)),
            _resident((None, D_MODEL, D_FF), lambda i: (l, 0, 0)),
            _resident((None, D_FF, D_MODEL), lambda i: (l, 0, 0)),
        ],
        out_specs=pl.BlockSpec((TM, D_MODEL), lambda i: (i, 0)),
        out_shape=jax.ShapeDtypeStruct((LP, D_MODEL), f32),
        compiler_params=_cparams(("parallel",)),
        name="ffn",
    )(h, g, wg, wu, wd)


def _inproj_kernel(x_ref, g_ref, w_ref, o_ref, us_ref):
    xn = _rmsnorm(x_ref[...], g_ref[...]).astype(bf16)
    for c0 in range(0, IN_COLS, TN_IN):
        res = jnp.dot(xn, w_ref[:, c0:c0 + TN_IN], preferred_element_type=f32)
        o_ref[:, c0:c0 + TN_IN] = res.astype(bf16)
        for s in range(N_SLAB):
            u0 = COL_U + s * LANES
            if c0 <= u0 < c0 + TN_IN:
                us_ref[s] = res[:, u0 - c0:u0 - c0 + LANES]


def _inproj(l, h, g, w):
    return pl.pallas_call(
        _inproj_kernel,
        grid=(LP // TM,),
        in_specs=[
            pl.BlockSpec((TM, D_MODEL), lambda i: (i, 0)),
            _resident((None, 1, D_MODEL), lambda i: (l, 0, 0)),
            _resident((None, D_MODEL, IN_COLS), lambda i: (l, 0, 0)),
        ],
        out_specs=[pl.BlockSpec((TM, IN_COLS), lambda i: (i, 0)),
                   pl.BlockSpec((N_SLAB, TM, LANES), lambda i: (0, i, 0))],
        out_shape=[jax.ShapeDtypeStruct((LP, IN_COLS), bf16),
                   jax.ShapeDtypeStruct((N_SLAB, LP, LANES), f32)],
        compiler_params=_cparams(("parallel",)),
        name="inproj",
    )(h, g, w)


def _attn_kernel(sink_ref, q_ref, kp_ref, kc_ref, kn_ref, km_ref,
                 vp_ref, vc_ref, vn_ref, vm_ref, bias_ref, o_ref):
    k_all, v_all = [], []
    for kh in range(N_KV_HEADS):
        sl = slice(kh * HEAD_DIM, (kh + 1) * HEAD_DIM)
        k_all.append(jnp.concatenate([kp_ref[:, sl], kc_ref[:, sl], kn_ref[:, sl], km_ref[:, sl]], axis=0))
        v_all.append(jnp.concatenate([vp_ref[:, sl], vc_ref[:, sl], vn_ref[:, sl], vm_ref[:, sl]], axis=0))

    def logits(h):
        qh = q_ref[:, h * HEAD_DIM:(h + 1) * HEAD_DIM]
        return lax.dot_general(qh, k_all[h // Q_GROUP], (((1,), (1,)), ((), ())),
                               preferred_element_type=f32) + bias_ref[0, h]

    def softmax(h, s):
        sink = sink_ref[h] * LOG2E
        m = jnp.maximum(jnp.max(s, axis=-1, keepdims=True), sink)
        p = jnp.exp2(s - m)
        denom = jnp.sum(p, axis=-1, keepdims=True) + jnp.exp2(sink - m)
        return p.astype(bf16), denom

    outs = []
    s_q = {0: logits(0), 1: logits(1)}
    p_q = {0: softmax(0, s_q.pop(0))}
    for h in range(N_HEADS):
        if h + 2 < N_HEADS:
            s_q[h + 2] = logits(h + 2)
        if h + 1 < N_HEADS:
            p_q[h + 1] = softmax(h + 1, s_q.pop(h + 1))
        p, denom = p_q.pop(h)
        o = jnp.dot(p, v_all[h // Q_GROUP], preferred_element_type=f32)
        outs.append(o / denom)
    o_ref[...] = jnp.concatenate(outs, axis=1).astype(bf16)


def _bias_variant(n):
    return jnp.where(n == 0, 0, jnp.where(n == 1, 1, jnp.where(n == NB - 1, 3, 2)))


def _attn(proj, sink, bias):
    kv = lambda col, fn: pl.BlockSpec((BLOCK, ATTN_KV), lambda n: (fn(n), col))
    prev = lambda n: jnp.maximum(n - 1, 0)
    cur = lambda n: n
    nxt = lambda n: jnp.minimum(n + 1, NB - 1)
    first = lambda n: 0
    ck, cv = COL_K // ATTN_KV, COL_V // ATTN_KV
    return pl.pallas_call(
        _attn_kernel,
        grid=(NB,),
        in_specs=[
            pl.BlockSpec(memory_space=pltpu.SMEM),
            pl.BlockSpec((BLOCK, ATTN_Q), lambda n: (n, 0)),
            kv(ck, prev), kv(ck, cur), kv(ck, nxt), kv(ck, first),
            kv(cv, prev), kv(cv, cur), kv(cv, nxt), kv(cv, first),
            pl.BlockSpec((1, N_HEADS, BLOCK, 4 * BLOCK), lambda n: (_bias_variant(n), 0, 0, 0)),
        ],
        out_specs=pl.BlockSpec((BLOCK, ATTN_Q), lambda n: (n, 0)),
        out_shape=jax.ShapeDtypeStruct((LP, ATTN_Q), bf16),
        compiler_params=_cparams(("parallel",)),
        name="attn",
    )(sink, proj, proj, proj, proj, proj, proj, proj, proj, proj, bias)


def _attn_bias():
    qi = np.arange(BLOCK)[:, None]
    sj = np.arange(3 * BLOCK)[None, :]
    dist = np.abs(qi + BLOCK - sj)
    slopes = 2.0 ** (-8.0 * np.arange(1, N_HEADS + 1) / N_HEADS)
    band = -LOG2E * slopes[:, None, None] * dist[None].astype(np.float64)
    in_win = (dist <= WINDOW)[None]
    meta = np.where(np.arange(BLOCK) >= PAD, 0.0, NEG)[None, None, :]
    meta = np.broadcast_to(meta, (N_HEADS, BLOCK, BLOCK))
    out = []
    for blk_ok in ((False, False, True), (False, True, True), (True, True, True), (True, True, False)):
        kvalid = np.repeat(np.asarray(blk_ok), BLOCK)[None, None, :]
        b = np.where(in_win & kvalid, band, NEG)
        out.append(np.concatenate([b, meta], axis=-1))
    return jnp.asarray(np.stack(out), dtype=f32)


def _cmul(ar, ai, br, bi):
    return ar * br - ai * bi, ar * bi + ai * br


def _expand_block_diag(src_ref, dst_ref, row_inner, col_inner):
    lane = lax.broadcasted_iota(jnp.int32, (LANES, CAT), 0)
    col = lax.broadcasted_iota(jnp.int32, (LANES, CAT), 1)
    src_lane = (col // (G_SLAB * col_inner)) * col_inner + col % col_inner
    spread = jnp.where(src_lane == lane, 1.0, 0.0).astype(bf16)
    col_g = (lax.broadcasted_iota(jnp.int32, (1, CAT), 1) // col_inner) % G_SLAB
    for r0 in range(0, CAT, EXPAND_ROWS):
        row_g = ((r0 + lax.broadcasted_iota(jnp.int32, (EXPAND_ROWS, 1), 0)) // row_inner) % G_SLAB
        full = jnp.dot(src_ref[r0:r0 + EXPAND_ROWS, :].astype(bf16), spread, preferred_element_type=f32)
        dst_ref[r0:r0 + EXPAND_ROWS, :] = jnp.where(row_g == col_g, full, 0.0).astype(bf16)


def _ssm_kernel(reverse, u_ref, wc_ref, mc_ref, vc_ref, ast_ref, pc_ref, y_ref,
                w_ref, m_ref, v_ref, ucat_ref, loc_ref, sp_ref, ycat_ref, carry_ref):
    @pl.when(pl.program_id(1) == 0)
    def _():
        carry_ref[...] = jnp.zeros_like(carry_ref)
        _expand_block_diag(wc_ref, w_ref, SSM_GROUP_CH, SSM_STATE)
        _expand_block_diag(mc_ref, m_ref, SSM_GROUP_CH, SSM_GROUP_CH)
        _expand_block_diag(vc_ref, v_ref, SSM_STATE, SSM_GROUP_CH)

    def rows8(k):
        return pl.ds(pl.multiple_of(k * SUBLANES, SUBLANES), SUBLANES)

    def token_rows(k, t):
        return pl.ds(k * (SUBLANES * CH_T) + t, SUBLANES, stride=CH_T)

    def gather(k, _):
        for t in range(CH_T):
            ucat_ref[rows8(k), t * LANES:(t + 1) * LANES] = u_ref[token_rows(k, t), :]
        return 0

    lax.fori_loop(0, R_VREG, gather, 0, unroll=2)
    ucat = ucat_ref[...].astype(bf16)
    loc_ref[...] = jnp.dot(ucat, w_ref[...], preferred_element_type=f32)
    ycat_ref[...] = jnp.dot(ucat, m_ref[...], preferred_element_type=f32)

    row = lax.broadcasted_iota(jnp.int32, (SUBLANES, SLAB_STATE), 0)
    edge = SUBLANES - 1 if reverse else 0
    last = 0 if reverse else SUBLANES - 1
    toward = lambda x, n: pltpu.roll(x, (SUBLANES - n) if reverse else n, 0)

    def scan(n, carry):
        k = R_VREG - 1 - n if reverse else n
        xr = loc_ref[rows8(k), :SLAB_STATE]
        xi = loc_ref[rows8(k), SLAB_STATE:]
        for j in range(3):
            tr, ti = _cmul(ast_ref[j, 0], ast_ref[j, 1], toward(xr, 1 << j), toward(xi, 1 << j))
            xr, xi = xr + tr, xi + ti
        cr, ci = carry
        tr, ti = _cmul(pc_ref[0], pc_ref[1], cr, ci)
        xr, xi = xr + tr, xi + ti
        sp_ref[rows8(k), :SLAB_STATE] = jnp.where(row == edge, cr, toward(xr, 1))
        sp_ref[rows8(k), SLAB_STATE:] = jnp.where(row == edge, ci, toward(xi, 1))
        bcast = lambda x: jnp.broadcast_to(x[last:last + 1], (SUBLANES, SLAB_STATE))
        return bcast(xr), bcast(xi)

    cr, ci = lax.fori_loop(0, R_VREG, scan, (carry_ref[0], carry_ref[1]))
    carry_ref[0] = cr
    carry_ref[1] = ci
    ycat_ref[...] += jnp.dot(sp_ref[...].astype(bf16), v_ref[...], preferred_element_type=f32)

    def scatter(k, _):
        for t in range(CH_T):
            y_ref[token_rows(k, t), :] = ycat_ref[rows8(k), t * LANES:(t + 1) * LANES]
        return 0

    lax.fori_loop(0, R_VREG, scatter, 0, unroll=2)


def _ssm(l, d, us, wc, mc, vc, ast, pc):
    tile = (lambda i: NT_SSM - 1 - i) if d else (lambda i: i)
    rows = pl.BlockSpec((None, T_SSM, LANES), lambda s, i: (s, tile(i), 0))
    mat = pl.BlockSpec((None, None, None, CAT, LANES), lambda s, i: (l, d, s, 0, 0))
    return pl.pallas_call(
        functools.partial(_ssm_kernel, bool(d)),
        grid=(N_SLAB, NT_SSM),
        in_specs=[
            rows, mat, mat, mat,
            pl.BlockSpec((None, None, None, 3, 2, SUBLANES, SLAB_STATE), lambda s, i: (l, d, s, 0, 0, 0, 0)),
            pl.BlockSpec((None, None, None, 2, SUBLANES, SLAB_STATE), lambda s, i: (l, d, s, 0, 0, 0)),
        ],
        out_specs=rows,
        out_shape=jax.ShapeDtypeStruct((N_SLAB, LP, LANES), f32),
        scratch_shapes=[
            pltpu.VMEM((CAT, CAT), bf16),
            pltpu.VMEM((CAT, CAT), bf16),
            pltpu.VMEM((CAT, CAT), bf16),
            pltpu.VMEM((R_SSM, CAT), f32),
            pltpu.VMEM((R_SSM, CAT), f32),
            pltpu.VMEM((R_SSM, CAT), f32),
            pltpu.VMEM((R_SSM, CAT), f32),
            pltpu.VMEM((2, SUBLANES, SLAB_STATE), f32),
        ],
        compiler_params=_cparams(("parallel", "arbitrary")),
        name="ssm",
    )(us, wc, mc, vc, ast, pc)


def _ssm_params(lam_re, lam_im, log_dt, b_re, b_im, c_re, c_im):
    hi = lax.Precision.HIGHEST
    dt = jnp.exp(log_dt)[..., None]

    def apow(n):
        mag = jnp.exp(lam_re * dt * n)
        return mag * jnp.cos(lam_im * dt * n), mag * jnp.sin(lam_im * dt * n)

    ar, ai = apow(1)
    den = lam_re * lam_re + lam_im * lam_im
    fr = ((ar - 1.0) * lam_re + ai * lam_im) / den
    fi = (ai * lam_re - (ar - 1.0) * lam_im) / den
    bbr = fr[..., None] * b_re - fi[..., None] * b_im
    bbi = fr[..., None] * b_im + fi[..., None] * b_re

    pows = [apow(n) for n in range(CH_T + 1)]
    abr = jnp.stack([p[0][..., None] * bbr - p[1][..., None] * bbi for p in pows[:CH_T]])
    abi = jnp.stack([p[0][..., None] * bbi + p[1][..., None] * bbr for p in pows[:CH_T]])
    car = jnp.stack([c_re * p[0][..., None, :] - c_im * p[1][..., None, :] for p in pows[1:]])
    cai = jnp.stack([c_re * p[1][..., None, :] + c_im * p[0][..., None, :] for p in pows[1:]])
    kk = (jnp.einsum('ldgop,nldgpi->nldgoi', c_re, abr, precision=hi)
          - jnp.einsum('ldgop,nldgpi->nldgoi', c_im, abi, precision=hi))

    def by_dir(x):
        return jnp.stack([x[:, :, 0], x[::-1][:, :, 1]], axis=2)

    slab = lambda x: x.reshape(x.shape[:3] + (N_SLAB, G_SLAB) + x.shape[4:])
    compact = lambda x: x.reshape(DEPTH, 2, N_SLAB, CAT, LANES)

    wpart = lambda ab: slab(by_dir(ab[::-1])).transpose(1, 2, 3, 0, 4, 6, 5)
    wc = compact(jnp.stack([wpart(abr), wpart(abi)], axis=-2))

    vpart = lambda ca: slab(by_dir(ca)).transpose(1, 2, 3, 4, 6, 0, 5)
    vc = compact(jnp.stack([vpart(car), -vpart(cai)], axis=3))

    s_idx, t_idx = np.arange(CH_T)[:, None], np.arange(CH_T)[None, :]
    causal = np.stack([s_idx <= t_idx, s_idx >= t_idx], axis=-1).astype(np.float32)
    x = kk[np.abs(t_idx - s_idx)] * causal[:, :, None, :, None, None, None]
    x = x.reshape(CH_T, CH_T, DEPTH, 2, N_SLAB, G_SLAB, SSM_GROUP_CH, SSM_GROUP_CH)
    mc = compact(x.transpose(2, 3, 4, 0, 5, 7, 1, 6))

    def lanes(n):
        r, i = apow(n)
        return jnp.stack([r, i], axis=2).reshape(DEPTH, 2, 2, N_SLAB, 1, SLAB_STATE)

    r8 = np.arange(SUBLANES)
    steps = []
    for j in range(3):
        sh = 1 << j
        has_partner = np.stack([r8 >= sh, r8 < SUBLANES - sh]).astype(np.float32)
        steps.append(lanes(CH_T * sh) * has_partner[None, :, None, None, :, None])
    ast = jnp.stack(steps, axis=2)
    ast = ast.transpose(0, 1, 4, 2, 3, 5, 6)
    per_row = jnp.concatenate([lanes(CH_T * (r + 1)) for r in range(SUBLANES)], axis=4)
    pc = jnp.stack([per_row[:, 0], per_row[:, 1, :, :, ::-1]], axis=1)
    pc = pc.transpose(0, 1, 3, 2, 4, 5)
    return wc, mc, vc, ast, pc


def _gelu_tanh(x):
    c = math.sqrt(2.0 / math.pi)
    return 0.5 * x * (1.0 + jnp.tanh(c * (x + 0.044715 * (x * x * x))))


def _merge_kernel(h_ref, yf_ref, yb_ref, us_ref, at_ref, gs_ref, ga_ref, d_ref,
                  wglu_ref, wbs_ref, wba_ref, wout_ref, o_ref):
    lanes = lambda ref: jnp.concatenate([ref[s] for s in range(N_SLAB)], axis=1)
    y = lanes(yf_ref) + lanes(yb_ref) + d_ref[...] * lanes(us_ref)
    z = _gelu_tanh(y)
    gl = jnp.dot(z.astype(bf16), wglu_ref[...], preferred_element_type=f32)
    ys = (z * jax.nn.sigmoid(gl)).astype(bf16)
    m1 = jnp.dot(ys, wbs_ref[...], preferred_element_type=f32)
    m2 = jnp.dot(at_ref[...], wba_ref[...], preferred_element_type=f32)
    merged = (jax.nn.sigmoid(gs_ref[...].astype(f32)) * m1
              + jax.nn.sigmoid(ga_ref[...].astype(f32)) * m2)
    out = h_ref[...] + jnp.dot(merged.astype(bf16), wout_ref[...], preferred_element_type=f32)
    rows = pl.program_id(0) * TM + lax.broadcasted_iota(jnp.int32, (TM, 1), 0)
    o_ref[...] = jnp.where(rows >= PAD, out, 0.0)


def _merge(l, h, yf, yb, us, proj, attn, d, wglu, wbs, wba, wout):
    row = lambda w, col: pl.BlockSpec((TM, w), lambda i: (i, col))
    slab = pl.BlockSpec((N_SLAB, TM, LANES), lambda i: (0, i, 0))
    full = lambda r, c: _resident((None, r, c), lambda i: (l, 0, 0))
    return pl.pallas_call(
        _merge_kernel,
        grid=(LP // TM,),
        in_specs=[
            row(D_MODEL, 0), slab, slab, slab, row(ATTN_Q, 0),
            row(D_MODEL, COL_GS // D_MODEL), row(D_MODEL, COL_GA // D_MODEL),
            full(1, SSM_WIDTH), full(SSM_WIDTH, SSM_WIDTH), full(SSM_WIDTH, D_MODEL),
            full(ATTN_Q, D_MODEL), full(D_MODEL, D_MODEL),
        ],
        out_specs=row(D_MODEL, 0),
        out_shape=jax.ShapeDtypeStruct((LP, D_MODEL), f32),
        compiler_params=_cparams(("parallel",)),
        name="merge",
    )(h, yf, yb, us, attn, proj, proj, d, wglu, wbs, wba, wout)


def _final_kernel(x_ref, g_ref, o_ref):
    o_ref[...] = _rmsnorm(x_ref[...], g_ref[...])


def _final_norm(h, g):
    return pl.pallas_call(
        _final_kernel,
        grid=(SEQ // T_FINAL,),
        in_specs=[pl.BlockSpec((pl.Element(T_FINAL), pl.Element(D_MODEL)),
                               lambda i: (pl.multiple_of((i * (T_FINAL // BLOCK) + 1) * BLOCK, BLOCK), 0)),
                  pl.BlockSpec((1, D_MODEL), lambda i: (0, 0))],
        out_specs=pl.BlockSpec((T_FINAL, D_MODEL), lambda i: (i, 0)),
        out_shape=jax.ShapeDtypeStruct((SEQ, D_MODEL), f32),
        compiler_params=_cparams(("parallel",)),
        name="final_norm",
    )(h, g)


def kernel(x, meta_tokens, ffn1_norm, ffn1_w_gate, ffn1_w_up, ffn1_w_down, mix_norm, w_in, ssm_lam_re, ssm_lam_im, ssm_log_dt, ssm_b_re, ssm_b_im, ssm_c_re, ssm_c_im, ssm_d, ssm_w_glu, attn_sink, w_branch_ssm, w_branch_attn, w_out, ffn2_norm, ffn2_w_gate, ffn2_w_up, ffn2_w_down, final_norm):
    assert x.shape == (1, SEQ, D_MODEL)
    h = jnp.concatenate([jnp.zeros((PAD, D_MODEL), f32), meta_tokens.astype(f32), x[0]], axis=0)

    col_scale = jnp.where(jnp.arange(IN_COLS) < ATTN_Q, LOG2E * HEAD_DIM ** -0.5, 1.0).astype(f32)
    w_in_b = (w_in * col_scale).astype(bf16)
    cast = lambda w: w.astype(bf16)
    f1g, f1u, f1d = cast(ffn1_w_gate), cast(ffn1_w_up), cast(ffn1_w_down)
    f2g, f2u, f2d = cast(ffn2_w_gate), cast(ffn2_w_up), cast(ffn2_w_down)
    wglu, wbs, wba, wo = cast(ssm_w_glu), cast(w_branch_ssm), cast(w_branch_attn), cast(w_out)
    ssm_mats = _ssm_params(ssm_lam_re, ssm_lam_im, ssm_log_dt, ssm_b_re, ssm_b_im, ssm_c_re, ssm_c_im)
    bias = _attn_bias()

    n1, nm, n2, dskip = (v[:, None, :] for v in (ffn1_norm, mix_norm, ffn2_norm, ssm_d))
    for l in range(DEPTH):
        h = _ffn(l, h, n1, f1g, f1u, f1d)
        proj, us = _inproj(l, h, nm, w_in_b)
        attn = _attn(proj, attn_sink[l], bias)
        yf = _ssm(l, 0, us, *ssm_mats)
        yb = _ssm(l, 1, us, *ssm_mats)
        h = _merge(l, h, yf, yb, us, proj, attn, dskip, wglu, wbs, wba, wo)
        h = _ffn(l, h, n2, f2g, f2u, f2d)
    return _final_norm(h, final_norm[None])[None]
```

```python
import functools
import math

import numpy as np
import jax
import jax.numpy as jnp
from jax import lax
from jax.experimental import pallas as pl
from jax.experimental.pallas import tpu as pltpu

f32 = jnp.float32
bf16 = jnp.bfloat16

D_MODEL = 1024
SEQ = 16384
DEPTH = 4
N_META = 16
N_HEADS = 16
N_KV_HEADS = 4
HEAD_DIM = 64
Q_GROUP = N_HEADS // N_KV_HEADS
WINDOW = 128
BLOCK = 128
SSM_WIDTH = 512
SSM_GROUP_CH = 16
SSM_GROUPS = 32
SSM_STATE = 64
D_FF = 2816
EPS = 1e-6
NEG = -1e30
LOG2E = math.log2(math.e)

ATTN_Q = N_HEADS * HEAD_DIM
ATTN_KV = N_KV_HEADS * HEAD_DIM
IN_COLS = 4096
COL_K = ATTN_Q
COL_V = ATTN_Q + ATTN_KV
COL_U = ATTN_Q + 2 * ATTN_KV
COL_GS = COL_U + SSM_WIDTH
COL_GA = COL_GS + D_MODEL

PAD = BLOCK - N_META
LP = PAD + N_META + SEQ
NB = LP // BLOCK

LANES = 128
SUBLANES = 8
VMEM_LIMIT = 56 * 1024 * 1024

TM = 688
MXU_TILE = 256
FF_CHUNK = 2 * MXU_TILE
TN_IN = 4 * MXU_TILE
T_FINAL = 1024

CH_T = SUBLANES
N_SLAB = SSM_WIDTH // LANES
G_SLAB = SSM_GROUPS // N_SLAB
SLAB_STATE = G_SLAB * SSM_STATE
CAT = CH_T * LANES
T_SSM = LP // 3
NT_SSM = LP // T_SSM
R_SSM = T_SSM // CH_T
R_VREG = R_SSM // SUBLANES
EXPAND_ROWS = 256


def _cparams(sem):
    return pltpu.CompilerParams(dimension_semantics=sem, vmem_limit_bytes=VMEM_LIMIT)


def _rmsnorm(x, g):
    ms = jnp.mean(x * x, axis=-1, keepdims=True)
    return x * lax.rsqrt(ms + EPS) * g


def _ffn_kernel(x_ref, g_ref, wg_ref, wu_ref, wd_ref, o_ref):
    x = x_ref[...]
    xn = _rmsnorm(x, g_ref[...]).astype(bf16)
    acc = None
    for c0 in range(0, D_FF, FF_CHUNK):
        c1 = min(c0 + FF_CHUNK, D_FF)
        a = jnp.dot(xn, wg_ref[:, c0:c1], preferred_element_type=f32)
        b = jnp.dot(xn, wu_ref[:, c0:c1], preferred_element_type=f32)
        mid = (a * jax.nn.sigmoid(a) * b).astype(bf16)
        part = jnp.dot(mid, wd_ref[c0:c1, :], preferred_element_type=f32)
        acc = part if acc is None else acc + part
    o_ref[...] = x + 0.5 * acc


def _resident(shape, index_map):
    return pl.BlockSpec(shape, index_map, pipeline_mode=pl.Buffered(1))


def _ffn(l, h, g, wg, wu, wd):
    return pl.pallas_call(
        _ffn_kernel,
        grid=(LP // TM,),
        in_specs=[
            pl.BlockSpec((TM, D_MODEL), lambda i: (i, 0)),
            _resident((None, 1, D_MODEL), lambda i: (l, 0, 0)),
            _resident((None, D_MODEL, D_FF), lambda i: (l, 0, 0)),
            _resident((None, D_MODEL, D_FF), lambda i: (l, 0, 0)),
            _resident((None, D_FF, D_MODEL), lambda i: (l, 0, 0)),
        ],
        out_specs=pl.BlockSpec((TM, D_MODEL), lambda i: (i, 0)),
        out_shape=jax.ShapeDtypeStruct((LP, D_MODEL), f32),
        compiler_params=_cparams(("parallel",)),
        name="ffn",
    )(h, g, wg, wu, wd)


def _inproj_kernel(x_ref, g_ref, w_ref, o_ref, us_ref):
    xn = _rmsnorm(x_ref[...], g_ref[...]).astype(bf16)
    for c0 in range(0, IN_COLS, TN_IN):
        res = jnp.dot(xn, w_ref[:, c0:c0 + TN_IN], preferred_element_type=f32)
        o_ref[:, c0:c0 + TN_IN] = res.astype(bf16)
        for s in range(N_SLAB):
            u0 = COL_U + s * LANES
            if c0 <= u0 < c0 + TN_IN:
                us_ref[s] = res[:, u0 - c0:u0 - c0 + LANES]


def _inproj(l, h, g, w):
    return pl.pallas_call(
        _inproj_kernel,
        grid=(LP // TM,),
        in_specs=[
            pl.BlockSpec((TM, D_MODEL), lambda i: (i, 0)),
            _resident((None, 1, D_MODEL), lambda i: (l, 0, 0)),
            _resident((None, D_MODEL, IN_COLS), lambda i: (l, 0, 0)),
        ],
        out_specs=[pl.BlockSpec((TM, IN_COLS), lambda i: (i, 0)),
                   pl.BlockSpec((N_SLAB, TM, LANES), lambda i: (0, i, 0))],
        out_shape=[jax.ShapeDtypeStruct((LP, IN_COLS), bf16),
                   jax.ShapeDtypeStruct((N_SLAB, LP, LANES), f32)],
        compiler_params=_cparams(("parallel",)),
        name="inproj",
    )(h, g, w)


def _attn_kernel(sink_ref, q_ref, kp_ref, kc_ref, kn_ref, km_ref,
                 vp_ref, vc_ref, vn_ref, vm_ref, bias_ref, o_ref):
    k_all, v_all = [], []
    for kh in range(N_KV_HEADS):
        sl = slice(kh * HEAD_DIM, (kh + 1) * HEAD_DIM)
        k_all.append(jnp.concatenate([kp_ref[:, sl], kc_ref[:, sl], kn_ref[:, sl], km_ref[:, sl]], axis=0))
        v_all.append(jnp.concatenate([vp_ref[:, sl], vc_ref[:, sl], vn_ref[:, sl], vm_ref[:, sl]], axis=0))

    def logits(h):
        qh = q_ref[:, h * HEAD_DIM:(h + 1) * HEAD_DIM]
        return lax.dot_general(qh, k_all[h // Q_GROUP], (((1,), (1,)), ((), ())),
                               preferred_element_type=f32) + bias_ref[0, h]

    def softmax(h, s):
        sink = sink_ref[h] * LOG2E
        m = jnp.maximum(jnp.max(s, axis=-1, keepdims=True), sink)
        p = jnp.exp2(s - m)
        denom = jnp.sum(p, axis=-1, keepdims=True) + jnp.exp2(sink - m)
        return p.astype(bf16), denom

    outs = []
    s_q = {0: logits(0), 1: logits(1)}
    p_q = {0: softmax(0, s_q.pop(0))}
    for h in range(N_HEADS):
        if h + 2 < N_HEADS:
            s_q[h + 2] = logits(h + 2)
        if h + 1 < N_HEADS:
            p_q[h + 1] = softmax(h + 1, s_q.pop(h + 1))
        p, denom = p_q.pop(h)
        o = jnp.dot(p, v_all[h // Q_GROUP], preferred_element_type=f32)
        outs.append(o / denom)
    o_ref[...] = jnp.concatenate(outs, axis=1).astype(bf16)


def _bias_variant(n):
    return jnp.where(n == 0, 0, jnp.where(n == 1, 1, jnp.where(n == NB - 1, 3, 2)))


def _attn(proj, sink, bias):
    kv = lambda col, fn: pl.BlockSpec((BLOCK, ATTN_KV), lambda n: (fn(n), col))
    prev = lambda n: jnp.maximum(n - 1, 0)
    cur = lambda n: n
    nxt = lambda n: jnp.minimum(n + 1, NB - 1)
    first = lambda n: 0
    ck, cv = COL_K // ATTN_KV, COL_V // ATTN_KV
    return pl.pallas_call(
        _attn_kernel,
        grid=(NB,),
        in_specs=[
            pl.BlockSpec(memory_space=pltpu.SMEM),
            pl.BlockSpec((BLOCK, ATTN_Q), lambda n: (n, 0)),
            kv(ck, prev), kv(ck, cur), kv(ck, nxt), kv(ck, first),
            kv(cv, prev), kv(cv, cur), kv(cv, nxt), kv(cv, first),
            pl.BlockSpec((1, N_HEADS, BLOCK, 4 * BLOCK), lambda n: (_bias_variant(n), 0, 0, 0)),
        ],
        out_specs=pl.BlockSpec((BLOCK, ATTN_Q), lambda n: (n, 0)),
        out_shape=jax.ShapeDtypeStruct((LP, ATTN_Q), bf16),
        compiler_params=_cparams(("parallel",)),
        name="attn",
    )(sink, proj, proj, proj, proj, proj, proj, proj, proj, proj, bias)


def _attn_bias():
    qi = np.arange(BLOCK)[:, None]
    sj = np.arange(3 * BLOCK)[None, :]
    dist = np.abs(qi + BLOCK - sj)
    slopes = 2.0 ** (-8.0 * np.arange(1, N_HEADS + 1) / N_HEADS)
    band = -LOG2E * slopes[:, None, None] * dist[None].astype(np.float64)
    in_win = (dist <= WINDOW)[None]
    meta = np.where(np.arange(BLOCK) >= PAD, 0.0, NEG)[None, None, :]
    meta = np.broadcast_to(meta, (N_HEADS, BLOCK, BLOCK))
    out = []
    for blk_ok in ((False, False, True), (False, True, True), (True, True, True), (True, True, False)):
        kvalid = np.repeat(np.asarray(blk_ok), BLOCK)[None, None, :]
        b = np.where(in_win & kvalid, band, NEG)
        out.append(np.concatenate([b, meta], axis=-1))
    return jnp.asarray(np.stack(out), dtype=f32)


def _cmul(ar, ai, br, bi):
    return ar * br - ai * bi, ar * bi + ai * br


def _expand_block_diag(src_ref, dst_ref, row_inner, col_inner):
    lane = lax.broadcasted_iota(jnp.int32, (LANES, CAT), 0)
    col = lax.broadcasted_iota(jnp.int32, (LANES, CAT), 1)
    src_lane = (col // (G_SLAB * col_inner)) * col_inner + col % col_inner
    spread = jnp.where(src_lane == lane, 1.0, 0.0).astype(bf16)
    col_g = (lax.broadcasted_iota(jnp.int32, (1, CAT), 1) // col_inner) % G_SLAB
    for r0 in range(0, CAT, EXPAND_ROWS):
        row_g = ((r0 + lax.broadcasted_iota(jnp.int32, (EXPAND_ROWS, 1), 0)) // row_inner) % G_SLAB
        full = jnp.dot(src_ref[r0:r0 + EXPAND_ROWS, :].astype(bf16), spread, preferred_element_type=f32)
        dst_ref[r0:r0 + EXPAND_ROWS, :] = jnp.where(row_g == col_g, full, 0.0).astype(bf16)


def _ssm_kernel(reverse, u_ref, wc_ref, vk_ref, vc_ref, ast_ref, pc_ref, y_ref,
                w_ref, m_ref, v_ref, ucat_ref, loc_ref, sp_ref, ycat_ref, carry_ref):
    @pl.when(pl.program_id(1) == 0)
    def _():
        carry_ref[...] = jnp.zeros_like(carry_ref)
        _expand_block_diag(wc_ref, w_ref, SSM_GROUP_CH, SSM_STATE)
        _expand_block_diag(vk_ref, v_ref, SSM_STATE, SSM_GROUP_CH)
        s0 = (0 if reverse else CH_T - 1) * LANES
        strip = jnp.dot(w_ref[s0:s0 + LANES, :], v_ref[...], preferred_element_type=f32).astype(bf16)
        for s in range(CH_T):
            lo, hi = (0, (s + 1) * LANES) if reverse else (s * LANES, CAT)
            src = (CH_T - 1 - s) * LANES if reverse else 0
            if lo > 0:
                m_ref[s * LANES:(s + 1) * LANES, :lo] = jnp.zeros((LANES, lo), bf16)
            m_ref[s * LANES:(s + 1) * LANES, lo:hi] = strip[:, src:src + hi - lo]
            if hi < CAT:
                m_ref[s * LANES:(s + 1) * LANES, hi:] = jnp.zeros((LANES, CAT - hi), bf16)
        _expand_block_diag(vc_ref, v_ref, SSM_STATE, SSM_GROUP_CH)

    def rows8(k):
        return pl.ds(pl.multiple_of(k * SUBLANES, SUBLANES), SUBLANES)

    def token_rows(k, t):
        return pl.ds(k * (SUBLANES * CH_T) + t, SUBLANES, stride=CH_T)

    def gather(k, _):
        for t in range(CH_T):
            ucat_ref[rows8(k), t * LANES:(t + 1) * LANES] = u_ref[token_rows(k, t), :]
        return 0

    lax.fori_loop(0, R_VREG, gather, 0, unroll=2)
    ucat = ucat_ref[...].astype(bf16)
    loc_ref[...] = jnp.dot(ucat, w_ref[...], preferred_element_type=f32)
    ycat_ref[...] = jnp.dot(ucat, m_ref[...], preferred_element_type=f32)

    row = lax.broadcasted_iota(jnp.int32, (SUBLANES, SLAB_STATE), 0)
    edge = SUBLANES - 1 if reverse else 0
    last = 0 if reverse else SUBLANES - 1
    toward = lambda x, n: pltpu.roll(x, (SUBLANES - n) if reverse else n, 0)

    def scan(n, carry):
        k = R_VREG - 1 - n if reverse else n
        xr = loc_ref[rows8(k), :SLAB_STATE]
        xi = loc_ref[rows8(k), SLAB_STATE:]
        for j in range(3):
            tr, ti = _cmul(ast_ref[j, 0], ast_ref[j, 1], toward(xr, 1 << j), toward(xi, 1 << j))
            xr, xi = xr + tr, xi + ti
        cr, ci = carry
        tr, ti = _cmul(pc_ref[0], pc_ref[1], cr, ci)
        xr, xi = xr + tr, xi + ti
        sp_ref[rows8(k), :SLAB_STATE] = jnp.where(row == edge, cr, toward(xr, 1))
        sp_ref[rows8(k), SLAB_STATE:] = jnp.where(row == edge, ci, toward(xi, 1))
        bcast = lambda x: jnp.broadcast_to(x[last:last + 1], (SUBLANES, SLAB_STATE))
        return bcast(xr), bcast(xi)

    cr, ci = lax.fori_loop(0, R_VREG, scan, (carry_ref[0], carry_ref[1]))
    carry_ref[0] = cr
    carry_ref[1] = ci
    ycat_ref[...] += jnp.dot(sp_ref[...].astype(bf16), v_ref[...], preferred_element_type=f32)

    def scatter(k, _):
        for t in range(CH_T):
            y_ref[token_rows(k, t), :] = ycat_ref[rows8(k), t * LANES:(t + 1) * LANES]
        return 0

    lax.fori_loop(0, R_VREG, scatter, 0, unroll=2)


def _ssm(l, d, us, wc, vk, vc, ast, pc):
    tile = (lambda i: NT_SSM - 1 - i) if d else (lambda i: i)
    rows = pl.BlockSpec((None, T_SSM, LANES), lambda s, i: (s, tile(i), 0))
    mat = pl.BlockSpec((None, None, None, CAT, LANES), lambda s, i: (l, d, s, 0, 0))
    return pl.pallas_call(
        functools.partial(_ssm_kernel, bool(d)),
        grid=(N_SLAB, NT_SSM),
        in_specs=[
            rows, mat, mat, mat,
            pl.BlockSpec((None, None, None, 3, 2, SUBLANES, SLAB_STATE), lambda s, i: (l, d, s, 0, 0, 0, 0)),
            pl.BlockSpec((None, None, None, 2, SUBLANES, SLAB_STATE), lambda s, i: (l, d, s, 0, 0, 0)),
        ],
        out_specs=rows,
        out_shape=jax.ShapeDtypeStruct((N_SLAB, LP, LANES), f32),
        scratch_shapes=[
            pltpu.VMEM((CAT, CAT), bf16),
            pltpu.VMEM((CAT, CAT), bf16),
            pltpu.VMEM((CAT, CAT), bf16),
            pltpu.VMEM((R_SSM, CAT), f32),
            pltpu.VMEM((R_SSM, CAT), f32),
            pltpu.VMEM((R_SSM, CAT), f32),
            pltpu.VMEM((R_SSM, CAT), f32),
            pltpu.VMEM((2, SUBLANES, SLAB_STATE), f32),
        ],
        compiler_params=_cparams(("parallel", "arbitrary")),
        name="ssm",
    )(us, wc, vk, vc, ast, pc)


def _ssm_params(lam_re, lam_im, log_dt, b_re, b_im, c_re, c_im):
    dt = jnp.exp(log_dt)[..., None]

    def apow(n):
        mag = jnp.exp(lam_re * dt * n)
        return mag * jnp.cos(lam_im * dt * n), mag * jnp.sin(lam_im * dt * n)

    pows = [apow(n) for n in range(CH_T + 1)]
    forward = (jnp.arange(2) == 0)[None, :, None, None]

    def pick(n_fwd, n_bwd):
        return (jnp.where(forward, pows[n_fwd][0], pows[n_bwd][0]),
                jnp.where(forward, pows[n_fwd][1], pows[n_bwd][1]))

    ar, ai = pows[1]
    den = lam_re * lam_re + lam_im * lam_im
    fr = ((ar - 1.0) * lam_re + ai * lam_im) / den
    fi = (ai * lam_re - (ar - 1.0) * lam_im) / den
    bt_re, bt_im = jnp.swapaxes(b_re, -1, -2), jnp.swapaxes(b_im, -1, -2)
    bbr = fr[..., None, :] * bt_re - fi[..., None, :] * bt_im
    bbi = fr[..., None, :] * bt_im + fi[..., None, :] * bt_re
    ct_re, ct_im = jnp.swapaxes(c_re, -1, -2), jnp.swapaxes(c_im, -1, -2)
    slab = lambda x: x.reshape((DEPTH, 2, N_SLAB, G_SLAB) + x.shape[3:])

    rows = []
    for s in range(CH_T):
        pr, pi = pick(CH_T - 1 - s, s)
        pr, pi = pr[..., None, :], pi[..., None, :]
        rows.append(slab(jnp.concatenate([pr * bbr - pi * bbi, pr * bbi + pi * bbr], axis=-1)))
    wc = jnp.stack(rows, axis=3).reshape(DEPTH, 2, N_SLAB, CAT, LANES)

    def c_times_powers(n_fwd, n_bwd):
        re, im = [], []
        for j in range(CH_T):
            pr, pi = pick(n_fwd[j], n_bwd[j])
            pr, pi = pr[..., None], pi[..., None]
            re.append(ct_re * pr - ct_im * pi)
            im.append(-(ct_re * pi + ct_im * pr))
        parts = [slab(jnp.concatenate(x, axis=-1)) for x in (re, im)]
        return jnp.stack(parts, axis=3).reshape(DEPTH, 2, N_SLAB, CAT, LANES)

    steps = list(range(CH_T))
    vk = c_times_powers(steps, steps[::-1])
    vc = c_times_powers([t + 1 for t in steps], [CH_T - t for t in steps])

    def lanes(n):
        r, i = apow(n)
        return jnp.stack([r, i], axis=2).reshape(DEPTH, 2, 2, N_SLAB, 1, SLAB_STATE)

    r8 = np.arange(SUBLANES)
    steps = []
    for j in range(3):
        sh = 1 << j
        has_partner = np.stack([r8 >= sh, r8 < SUBLANES - sh]).astype(np.float32)
        steps.append(lanes(CH_T * sh) * has_partner[None, :, None, None, :, None])
    ast = jnp.stack(steps, axis=2)
    ast = ast.transpose(0, 1, 4, 2, 3, 5, 6)
    per_row = jnp.concatenate([lanes(CH_T * (r + 1)) for r in range(SUBLANES)], axis=4)
    pc = jnp.stack([per_row[:, 0], per_row[:, 1, :, :, ::-1]], axis=1)
    pc = pc.transpose(0, 1, 3, 2, 4, 5)
    return wc, vk, vc, ast, pc


def _gelu_tanh(x):
    c = math.sqrt(2.0 / math.pi)
    return 0.5 * x * (1.0 + jnp.tanh(c * (x + 0.044715 * (x * x * x))))


def _merge_kernel(h_ref, yf_ref, yb_ref, us_ref, at_ref, gs_ref, ga_ref, d_ref,
                  wglu_ref, wbs_ref, wba_ref, wout_ref, o_ref):
    lanes = lambda ref: jnp.concatenate([ref[s] for s in range(N_SLAB)], axis=1)
    y = lanes(yf_ref) + lanes(yb_ref) + d_ref[...] * lanes(us_ref)
    z = _gelu_tanh(y)
    gl = jnp.dot(z.astype(bf16), wglu_ref[...], preferred_element_type=f32)
    ys = (z * jax.nn.sigmoid(gl)).astype(bf16)
    m1 = jnp.dot(ys, wbs_ref[...], preferred_element_type=f32)
    m2 = jnp.dot(at_ref[...], wba_ref[...], preferred_element_type=f32)
    merged = (jax.nn.sigmoid(gs_ref[...].astype(f32)) * m1
              + jax.nn.sigmoid(ga_ref[...].astype(f32)) * m2)
    out = h_ref[...] + jnp.dot(merged.astype(bf16), wout_ref[...], preferred_element_type=f32)
    rows = pl.program_id(0) * TM + lax.broadcasted_iota(jnp.int32, (TM, 1), 0)
    o_ref[...] = jnp.where(rows >= PAD, out, 0.0)


def _merge(l, h, yf, yb, us, proj, attn, d, wglu, wbs, wba, wout):
    row = lambda w, col: pl.BlockSpec((TM, w), lambda i: (i, col))
    slab = pl.BlockSpec((N_SLAB, TM, LANES), lambda i: (0, i, 0))
    full = lambda r, c: _resident((None, r, c), lambda i: (l, 0, 0))
    return pl.pallas_call(
        _merge_kernel,
        grid=(LP // TM,),
        in_specs=[
            row(D_MODEL, 0), slab, slab, slab, row(ATTN_Q, 0),
            row(D_MODEL, COL_GS // D_MODEL), row(D_MODEL, COL_GA // D_MODEL),
            full(1, SSM_WIDTH), full(SSM_WIDTH, SSM_WIDTH), full(SSM_WIDTH, D_MODEL),
            full(ATTN_Q, D_MODEL), full(D_MODEL, D_MODEL),
        ],
        out_specs=row(D_MODEL, 0),
        out_shape=jax.ShapeDtypeStruct((LP, D_MODEL), f32),
        compiler_params=_cparams(("parallel",)),
        name="merge",
    )(h, yf, yb, us, attn, proj, proj, d, wglu, wbs, wba, wout)


def _final_kernel(x_ref, g_ref, o_ref):
    o_ref[...] = _rmsnorm(x_ref[...], g_ref[...])


def _final_norm(h, g):
    return pl.pallas_call(
        _final_kernel,
        grid=(SEQ // T_FINAL,),
        in_specs=[pl.BlockSpec((pl.Element(T_FINAL), pl.Element(D_MODEL)),
                               lambda i: (pl.multiple_of((i * (T_FINAL // BLOCK) + 1) * BLOCK, BLOCK), 0)),
                  pl.BlockSpec((1, D_MODEL), lambda i: (0, 0))],
        out_specs=pl.BlockSpec((T_FINAL, D_MODEL), lambda i: (i, 0)),
        out_shape=jax.ShapeDtypeStruct((SEQ, D_MODEL), f32),
        compiler_params=_cparams(("parallel",)),
        name="final_norm",
    )(h, g)


def kernel(x, meta_tokens, ffn1_norm, ffn1_w_gate, ffn1_w_up, ffn1_w_down, mix_norm, w_in, ssm_lam_re, ssm_lam_im, ssm_log_dt, ssm_b_re, ssm_b_im, ssm_c_re, ssm_c_im, ssm_d, ssm_w_glu, attn_sink, w_branch_ssm, w_branch_attn, w_out, ffn2_norm, ffn2_w_gate, ffn2_w_up, ffn2_w_down, final_norm):
    assert x.shape == (1, SEQ, D_MODEL)
    h = jnp.concatenate([jnp.zeros((PAD, D_MODEL), f32), meta_tokens.astype(f32), x[0]], axis=0)

    col_scale = jnp.where(jnp.arange(IN_COLS) < ATTN_Q, LOG2E * HEAD_DIM ** -0.5, 1.0).astype(f32)
    w_in_b = (w_in * col_scale).astype(bf16)
    cast = lambda w: w.astype(bf16)
    f1g, f1u, f1d = cast(ffn1_w_gate), cast(ffn1_w_up), cast(ffn1_w_down)
    f2g, f2u, f2d = cast(ffn2_w_gate), cast(ffn2_w_up), cast(ffn2_w_down)
    wglu, wbs, wba, wo = cast(ssm_w_glu), cast(w_branch_ssm), cast(w_branch_attn), cast(w_out)
    ssm_mats = _ssm_params(ssm_lam_re, ssm_lam_im, ssm_log_dt, ssm_b_re, ssm_b_im, ssm_c_re, ssm_c_im)
    bias = _attn_bias()

    n1, nm, n2, dskip = (v[:, None, :] for v in (ffn1_norm, mix_norm, ffn2_norm, ssm_d))
    for l in range(DEPTH):
        h = _ffn(l, h, n1, f1g, f1u, f1d)
        proj, us = _inproj(l, h, nm, w_in_b)
        attn = _attn(proj, attn_sink[l], bias)
        yf = _ssm(l, 0, us, *ssm_mats)
        yb = _ssm(l, 1, us, *ssm_mats)
        h = _merge(l, h, yf, yb, us, proj, attn, dskip, wglu, wbs, wba, wo)
        h = _ffn(l, h, n2, f2g, f2u, f2d)
    return _final_norm(h, final_norm[None])[None]
```

```python
import functools
import math

import numpy as np
import jax
import jax.numpy as jnp
from jax import lax
from jax.experimental import pallas as pl
from jax.experimental.pallas import tpu as pltpu

f32 = jnp.float32
bf16 = jnp.bfloat16

D_MODEL = 1024
SEQ = 16384
DEPTH = 4
N_META = 16
N_HEADS = 16
N_KV_HEADS = 4
HEAD_DIM = 64
Q_GROUP = N_HEADS // N_KV_HEADS
WINDOW = 128
BLOCK = 128
SSM_WIDTH = 512
SSM_GROUP_CH = 16
SSM_GROUPS = 32
SSM_STATE = 64
D_FF = 2816
EPS = 1e-6
NEG = -1e30
LOG2E = math.log2(math.e)

ATTN_Q = N_HEADS * HEAD_DIM
ATTN_KV = N_KV_HEADS * HEAD_DIM
IN_COLS = 4096
COL_K = ATTN_Q
COL_V = ATTN_Q + ATTN_KV
COL_U = ATTN_Q + 2 * ATTN_KV
COL_GS = COL_U + SSM_WIDTH
COL_GA = COL_GS + D_MODEL

PAD = BLOCK - N_META
LP = PAD + N_META + SEQ
NB = LP // BLOCK

LANES = 128
SUBLANES = 8
VMEM_LIMIT = 56 * 1024 * 1024

TM = 688
TM_FFN = TM
MXU_TILE = 256
FF_CHUNK = 2 * MXU_TILE
TN_IN = 4 * MXU_TILE
T_FINAL = 512
HEADS_PER_DOT = 2
QB_STEP = 3

CH_T = SUBLANES
N_SLAB = SSM_WIDTH // LANES
G_SLAB = SSM_GROUPS // N_SLAB
SLAB_STATE = G_SLAB * SSM_STATE
CAT = CH_T * LANES
T_SSM = LP // 3
NT_SSM = LP // T_SSM
R_SSM = T_SSM // CH_T
R_VREG = R_SSM // SUBLANES
EXPAND_ROWS = 256


def _cparams(sem):
    return pltpu.CompilerParams(dimension_semantics=sem, vmem_limit_bytes=VMEM_LIMIT)


def _rmsnorm(x, g):
    ms = jnp.mean(x * x, axis=-1, keepdims=True)
    return x * lax.rsqrt(ms + EPS) * g


def _ffn_kernel(x_ref, g_ref, wg_ref, wu_ref, wd_ref, *rest):
    o_ref = rest[-1]
    x = x_ref[...]
    xn = _rmsnorm(x, g_ref[...]).astype(bf16)
    acc = None
    for c0 in range(0, D_FF, FF_CHUNK):
        c1 = min(c0 + FF_CHUNK, D_FF)
        a = jnp.dot(xn, wg_ref[:, c0:c1], preferred_element_type=f32)
        b = jnp.dot(xn, wu_ref[:, c0:c1], preferred_element_type=f32)
        mid = (a * jax.nn.sigmoid(a) * b).astype(bf16)
        part = jnp.dot(mid, wd_ref[c0:c1, :], preferred_element_type=f32)
        acc = part if acc is None else acc + part
    out = x + 0.5 * acc
    o_ref[...] = _rmsnorm(out, rest[0][...]) if len(rest) == 2 else out


def _resident(shape, index_map):
    return pl.BlockSpec(shape, index_map, pipeline_mode=pl.Buffered(1))


def _ffn(l, h, g, wg, wu, wd):
    return pl.pallas_call(
        _ffn_kernel,
        grid=(LP // TM_FFN,),
        in_specs=[
            pl.BlockSpec((TM_FFN, D_MODEL), lambda i: (i, 0)),
            _resident((None, 1, D_MODEL), lambda i: (l, 0, 0)),
            _resident((None, D_MODEL, D_FF), lambda i: (l, 0, 0)),
            _resident((None, D_MODEL, D_FF), lambda i: (l, 0, 0)),
            _resident((None, D_FF, D_MODEL), lambda i: (l, 0, 0)),
        ],
        out_specs=pl.BlockSpec((TM_FFN, D_MODEL), lambda i: (i, 0)),
        out_shape=jax.ShapeDtypeStruct((LP, D_MODEL), f32),
        compiler_params=_cparams(("parallel",)),
        name="ffn",
    )(h, g, wg, wu, wd)


def _ffn_final(l, h, g, wg, wu, wd, g_final):
    weights = lambda r, c: _resident((None, r, c), lambda i: (l, 0, 0))
    return pl.pallas_call(
        _ffn_kernel,
        grid=(SEQ // T_FINAL,),
        in_specs=[
            pl.BlockSpec((pl.Element(T_FINAL), pl.Element(D_MODEL)),
                         lambda i: (pl.multiple_of(BLOCK + i * T_FINAL, BLOCK), 0)),
            weights(1, D_MODEL), weights(D_MODEL, D_FF), weights(D_MODEL, D_FF), weights(D_FF, D_MODEL),
            _resident((1, D_MODEL), lambda i: (0, 0)),
        ],
        out_specs=pl.BlockSpec((T_FINAL, D_MODEL), lambda i: (i, 0)),
        out_shape=jax.ShapeDtypeStruct((SEQ, D_MODEL), f32),
        compiler_params=_cparams(("parallel",)),
        name="ffn_final",
    )(h, g, wg, wu, wd, g_final)


def _inproj_kernel(x_ref, g_ref, w_ref, o_ref, us_ref):
    xn = _rmsnorm(x_ref[...], g_ref[...]).astype(bf16)
    for c0 in range(0, IN_COLS, TN_IN):
        res = jnp.dot(xn, w_ref[:, c0:c0 + TN_IN], preferred_element_type=f32)
        o_ref[:, c0:c0 + TN_IN] = res.astype(bf16)
        for s in range(N_SLAB):
            u0 = COL_U + s * LANES
            if c0 <= u0 < c0 + TN_IN:
                us_ref[s] = res[:, u0 - c0:u0 - c0 + LANES]


def _inproj(l, h, g, w):
    return pl.pallas_call(
        _inproj_kernel,
        grid=(LP // TM,),
        in_specs=[
            pl.BlockSpec((TM, D_MODEL), lambda i: (i, 0)),
            _resident((None, 1, D_MODEL), lambda i: (l, 0, 0)),
            _resident((None, D_MODEL, IN_COLS), lambda i: (l, 0, 0)),
        ],
        out_specs=[pl.BlockSpec((TM, IN_COLS), lambda i: (i, 0)),
                   pl.BlockSpec((N_SLAB, TM, LANES), lambda i: (0, i, 0))],
        out_shape=[jax.ShapeDtypeStruct((LP, IN_COLS), bf16),
                   jax.ShapeDtypeStruct((N_SLAB, LP, LANES), f32)],
        compiler_params=_cparams(("parallel",)),
        name="inproj",
    )(h, g, w)


def _attn_block(sink_ref, q_ref, k_refs, v_refs, bias_ref, o_ref):
    k_all, v_all = [], []
    for kh in range(N_KV_HEADS):
        sl = slice(kh * HEAD_DIM, (kh + 1) * HEAD_DIM)
        k_all.append(jnp.concatenate([r[:, sl] for r in k_refs], axis=0))
        v_all.append(jnp.concatenate([r[:, sl] for r in v_refs], axis=0))

    def logits(t):
        heads = range(t * HEADS_PER_DOT, (t + 1) * HEADS_PER_DOT)
        q = jnp.concatenate([q_ref[:, h * HEAD_DIM:(h + 1) * HEAD_DIM] for h in heads], axis=0)
        return lax.dot_general(q, k_all[heads[0] // Q_GROUP], (((1,), (1,)), ((), ())),
                               preferred_element_type=f32)

    def softmax(t, s):
        ps, denoms = [], []
        for j in range(HEADS_PER_DOT):
            h = t * HEADS_PER_DOT + j
            sh = s[j * BLOCK:(j + 1) * BLOCK] + bias_ref[0, h]
            sink = sink_ref[h] * LOG2E
            m = jnp.maximum(jnp.max(sh, axis=-1, keepdims=True), sink)
            p = jnp.exp2(sh - m)
            denoms.append(jnp.sum(p, axis=-1, keepdims=True) + jnp.exp2(sink - m))
            ps.append(p.astype(bf16))
        return jnp.concatenate(ps, axis=0), denoms

    def finish(t, p, denoms):
        o = jnp.dot(p, v_all[t * HEADS_PER_DOT // Q_GROUP], preferred_element_type=f32)
        outs = [o[j * BLOCK:(j + 1) * BLOCK] / denoms[j] for j in range(HEADS_PER_DOT)]
        w = HEADS_PER_DOT * HEAD_DIM
        o_ref[:, t * w:(t + 1) * w] = jnp.concatenate(outs, axis=1).astype(bf16)

    return logits, softmax, finish


def _attn_kernel(sink_ref, q_ref, *refs):
    n_kv = QB_STEP + 2
    k_refs, km_ref = refs[:n_kv], refs[n_kv]
    v_refs, vm_ref = refs[n_kv + 1:2 * n_kv + 1], refs[2 * n_kv + 1]
    bias_refs = refs[2 * n_kv + 2:2 * n_kv + 2 + QB_STEP]
    o_ref = refs[-1]
    blocks = []
    for i in range(QB_STEP):
        rows = slice(i * BLOCK, (i + 1) * BLOCK)
        blocks.append(_attn_block(sink_ref, q_ref.at[rows], list(k_refs[i:i + 3]) + [km_ref],
                                  list(v_refs[i:i + 3]) + [vm_ref], bias_refs[i], o_ref.at[rows]))

    per_block = N_HEADS // HEADS_PER_DOT
    work = [(blk, t) for blk in blocks for t in range(per_block)]
    logits = lambda n: work[n][0][0](work[n][1])
    softmax = lambda n, s: work[n][0][1](work[n][1], s)
    s_q = {0: logits(0), 1: logits(1)}
    p_q = {0: softmax(0, s_q.pop(0))}
    for n in range(len(work)):
        if n + 2 < len(work):
            s_q[n + 2] = logits(n + 2)
        if n + 1 < len(work):
            p_q[n + 1] = softmax(n + 1, s_q.pop(n + 1))
        work[n][0][2](work[n][1], *p_q.pop(n))


def _attn(proj, sink, bias):
    n_steps = NB // QB_STEP
    clamp = lambda b: jnp.clip(b, 0, NB - 1)
    kv = lambda col, off: pl.BlockSpec((BLOCK, ATTN_KV), lambda n: (clamp(n * QB_STEP + off), col))
    first = lambda col: pl.BlockSpec((BLOCK, ATTN_KV), lambda n: (0, col))
    ck, cv = COL_K // ATTN_KV, COL_V // ATTN_KV
    offs = range(-1, QB_STEP + 1)

    def bias_spec(i):
        def variant(n):
            b = n * QB_STEP + i
            return jnp.where(b == 0, 0, jnp.where(b == 1, 1, jnp.where(b == NB - 1, 3, 2)))
        return pl.BlockSpec((1, N_HEADS, BLOCK, 4 * BLOCK), lambda n: (variant(n), 0, 0, 0),
                            pipeline_mode=pl.Buffered(1))

    return pl.pallas_call(
        _attn_kernel,
        grid=(n_steps,),
        in_specs=[pl.BlockSpec(memory_space=pltpu.SMEM),
                  pl.BlockSpec((QB_STEP * BLOCK, ATTN_Q), lambda n: (n, 0))]
                 + [kv(ck, o) for o in offs] + [first(ck)]
                 + [kv(cv, o) for o in offs] + [first(cv)]
                 + [bias_spec(i) for i in range(QB_STEP)],
        out_specs=pl.BlockSpec((QB_STEP * BLOCK, ATTN_Q), lambda n: (n, 0)),
        out_shape=jax.ShapeDtypeStruct((LP, ATTN_Q), bf16),
        compiler_params=_cparams(("parallel",)),
        name="attn",
    )(sink, proj, *([proj] * (2 * (QB_STEP + 3))), *([bias] * QB_STEP))


def _attn_bias():
    qi = np.arange(BLOCK)[:, None]
    sj = np.arange(3 * BLOCK)[None, :]
    dist = np.abs(qi + BLOCK - sj)
    slopes = 2.0 ** (-8.0 * np.arange(1, N_HEADS + 1) / N_HEADS)
    band = -LOG2E * slopes[:, None, None] * dist[None].astype(np.float64)
    in_win = (dist <= WINDOW)[None]
    meta = np.where(np.arange(BLOCK) >= PAD, 0.0, NEG)[None, None, :]
    meta = np.broadcast_to(meta, (N_HEADS, BLOCK, BLOCK))
    out = []
    for blk_ok in ((False, False, True), (False, True, True), (True, True, True), (True, True, False)):
        kvalid = np.repeat(np.asarray(blk_ok), BLOCK)[None, None, :]
        b = np.where(in_win & kvalid, band, NEG)
        out.append(np.concatenate([b, meta], axis=-1))
    return jnp.asarray(np.stack(out), dtype=f32)


def _cmul(ar, ai, br, bi):
    return ar * br - ai * bi, ar * bi + ai * br


def _expand_block_diag(src_ref, dst_ref, row_inner, col_inner):
    lane = lax.broadcasted_iota(jnp.int32, (LANES, CAT), 0)
    col = lax.broadcasted_iota(jnp.int32, (LANES, CAT), 1)
    src_lane = (col // (G_SLAB * col_inner)) * col_inner + col % col_inner
    spread = jnp.where(src_lane == lane, 1.0, 0.0).astype(bf16)
    col_g = (lax.broadcasted_iota(jnp.int32, (1, CAT), 1) // col_inner) % G_SLAB
    for r0 in range(0, CAT, EXPAND_ROWS):
        row_g = ((r0 + lax.broadcasted_iota(jnp.int32, (EXPAND_ROWS, 1), 0)) // row_inner) % G_SLAB
        full = jnp.dot(src_ref[r0:r0 + EXPAND_ROWS, :].astype(bf16), spread, preferred_element_type=f32)
        dst_ref[r0:r0 + EXPAND_ROWS, :] = jnp.where(row_g == col_g, full, 0.0).astype(bf16)


def _ssm_kernel(reverse, u_ref, wc_ref, vk_ref, vc_ref, ast_ref, pc_ref, y_ref,
                w_ref, m_ref, v_ref, ucat_ref, loc_ref, sp_ref, ycat_ref, carry_ref):
    @pl.when(pl.program_id(1) == 0)
    def _():
        carry_ref[...] = jnp.zeros_like(carry_ref)
        _expand_block_diag(wc_ref, w_ref, SSM_GROUP_CH, SSM_STATE)
        _expand_block_diag(vk_ref, v_ref, SSM_STATE, SSM_GROUP_CH)
        s0 = (0 if reverse else CH_T - 1) * LANES
        strip = jnp.dot(w_ref[s0:s0 + LANES, :], v_ref[...], preferred_element_type=f32).astype(bf16)
        for s in range(CH_T):
            lo, hi = (0, (s + 1) * LANES) if reverse else (s * LANES, CAT)
            src = (CH_T - 1 - s) * LANES if reverse else 0
            if lo > 0:
                m_ref[s * LANES:(s + 1) * LANES, :lo] = jnp.zeros((LANES, lo), bf16)
            m_ref[s * LANES:(s + 1) * LANES, lo:hi] = strip[:, src:src + hi - lo]
            if hi < CAT:
                m_ref[s * LANES:(s + 1) * LANES, hi:] = jnp.zeros((LANES, CAT - hi), bf16)
        _expand_block_diag(vc_ref, v_ref, SSM_STATE, SSM_GROUP_CH)

    def rows8(k):
        return pl.ds(pl.multiple_of(k * SUBLANES, SUBLANES), SUBLANES)

    def token_rows(k, t):
        return pl.ds(k * (SUBLANES * CH_T) + t, SUBLANES, stride=CH_T)

    def gather(k, _):
        for t in range(CH_T):
            ucat_ref[rows8(k), t * LANES:(t + 1) * LANES] = u_ref[token_rows(k, t), :]
        return 0

    lax.fori_loop(0, R_VREG, gather, 0, unroll=True)
    ucat = ucat_ref[...].astype(bf16)
    loc_ref[...] = jnp.dot(ucat, w_ref[...], preferred_element_type=f32)
    for c0 in range(0, CAT, MXU_TILE):
        k = slice(c0, CAT) if reverse else slice(0, c0 + MXU_TILE)
        ycat_ref[:, c0:c0 + MXU_TILE] = jnp.dot(ucat[:, k], m_ref[k, c0:c0 + MXU_TILE],
                                                preferred_element_type=f32)

    row = lax.broadcasted_iota(jnp.int32, (SUBLANES, SLAB_STATE), 0)
    edge = SUBLANES - 1 if reverse else 0
    last = 0 if reverse else SUBLANES - 1
    toward = lambda x, n: pltpu.roll(x, (SUBLANES - n) if reverse else n, 0)

    def scan(n, carry):
        k = R_VREG - 1 - n if reverse else n
        xr = loc_ref[rows8(k), :SLAB_STATE]
        xi = loc_ref[rows8(k), SLAB_STATE:]
        for j in range(3):
            tr, ti = _cmul(ast_ref[j, 0], ast_ref[j, 1], toward(xr, 1 << j), toward(xi, 1 << j))
            xr, xi = xr + tr, xi + ti
        cr, ci = carry
        tr, ti = _cmul(pc_ref[0], pc_ref[1], cr, ci)
        xr, xi = xr + tr, xi + ti
        sp_ref[rows8(k), :SLAB_STATE] = jnp.where(row == edge, cr, toward(xr, 1))
        sp_ref[rows8(k), SLAB_STATE:] = jnp.where(row == edge, ci, toward(xi, 1))
        bcast = lambda x: jnp.broadcast_to(x[last:last + 1], (SUBLANES, SLAB_STATE))
        return bcast(xr), bcast(xi)

    cr, ci = lax.fori_loop(0, R_VREG, scan, (carry_ref[0], carry_ref[1]), unroll=True)
    carry_ref[0] = cr
    carry_ref[1] = ci
    ycat_ref[...] += jnp.dot(sp_ref[...].astype(bf16), v_ref[...], preferred_element_type=f32)

    def scatter(k, _):
        for t in range(CH_T):
            y_ref[token_rows(k, t), :] = ycat_ref[rows8(k), t * LANES:(t + 1) * LANES]
        return 0

    lax.fori_loop(0, R_VREG, scatter, 0, unroll=True)


def _ssm(l, d, us, wc, vk, vc, ast, pc):
    tile = (lambda i: NT_SSM - 1 - i) if d else (lambda i: i)
    rows = pl.BlockSpec((None, T_SSM, LANES), lambda s, i: (s, tile(i), 0))
    mat = pl.BlockSpec((None, None, None, CAT, LANES), lambda s, i: (l, d, s, 0, 0))
    return pl.pallas_call(
        functools.partial(_ssm_kernel, bool(d)),
        grid=(N_SLAB, NT_SSM),
        in_specs=[
            rows, mat, mat, mat,
            pl.BlockSpec((None, None, None, 3, 2, SUBLANES, SLAB_STATE), lambda s, i: (l, d, s, 0, 0, 0, 0)),
            pl.BlockSpec((None, None, None, 2, SUBLANES, SLAB_STATE), lambda s, i: (l, d, s, 0, 0, 0)),
        ],
        out_specs=rows,
        out_shape=jax.ShapeDtypeStruct((N_SLAB, LP, LANES), f32),
        scratch_shapes=[
            pltpu.VMEM((CAT, CAT), bf16),
            pltpu.VMEM((CAT, CAT), bf16),
            pltpu.VMEM((CAT, CAT), bf16),
            pltpu.VMEM((R_SSM, CAT), f32),
            pltpu.VMEM((R_SSM, CAT), f32),
            pltpu.VMEM((R_SSM, CAT), f32),
            pltpu.VMEM((R_SSM, CAT), f32),
            pltpu.VMEM((2, SUBLANES, SLAB_STATE), f32),
        ],
        compiler_params=_cparams(("parallel", "arbitrary")),
        name="ssm",
    )(us, wc, vk, vc, ast, pc)


def _ssm_params(lam_re, lam_im, log_dt, b_re, b_im, c_re, c_im):
    dt = jnp.exp(log_dt)[..., None]

    def apow(n):
        mag = jnp.exp(lam_re * dt * n)
        return mag * jnp.cos(lam_im * dt * n), mag * jnp.sin(lam_im * dt * n)

    pows = [apow(n) for n in range(CH_T + 1)]
    forward = (jnp.arange(2) == 0)[None, :, None, None]

    def pick(n_fwd, n_bwd):
        return (jnp.where(forward, pows[n_fwd][0], pows[n_bwd][0]),
                jnp.where(forward, pows[n_fwd][1], pows[n_bwd][1]))

    ar, ai = pows[1]
    den = lam_re * lam_re + lam_im * lam_im
    fr = ((ar - 1.0) * lam_re + ai * lam_im) / den
    fi = (ai * lam_re - (ar - 1.0) * lam_im) / den
    bt_re, bt_im = jnp.swapaxes(b_re, -1, -2), jnp.swapaxes(b_im, -1, -2)
    bbr = fr[..., None, :] * bt_re - fi[..., None, :] * bt_im
    bbi = fr[..., None, :] * bt_im + fi[..., None, :] * bt_re
    ct_re, ct_im = jnp.swapaxes(c_re, -1, -2), jnp.swapaxes(c_im, -1, -2)
    slab = lambda x: x.reshape((DEPTH, 2, N_SLAB, G_SLAB) + x.shape[3:])

    rows = []
    for s in range(CH_T):
        pr, pi = pick(CH_T - 1 - s, s)
        pr, pi = pr[..., None, :], pi[..., None, :]
        rows.append(slab(jnp.concatenate([pr * bbr - pi * bbi, pr * bbi + pi * bbr], axis=-1)))
    wc = jnp.stack(rows, axis=3).reshape(DEPTH, 2, N_SLAB, CAT, LANES)

    def c_times_powers(n_fwd, n_bwd):
        re, im = [], []
        for j in range(CH_T):
            pr, pi = pick(n_fwd[j], n_bwd[j])
            pr, pi = pr[..., None], pi[..., None]
            re.append(ct_re * pr - ct_im * pi)
            im.append(-(ct_re * pi + ct_im * pr))
        parts = [slab(jnp.concatenate(x, axis=-1)) for x in (re, im)]
        return jnp.stack(parts, axis=3).reshape(DEPTH, 2, N_SLAB, CAT, LANES)

    steps = list(range(CH_T))
    vk = c_times_powers(steps, steps[::-1])
    vc = c_times_powers([t + 1 for t in steps], [CH_T - t for t in steps])

    def lanes(n):
        r, i = apow(n)
        return jnp.stack([r, i], axis=2).reshape(DEPTH, 2, 2, N_SLAB, 1, SLAB_STATE)

    r8 = np.arange(SUBLANES)
    steps = []
    for j in range(3):
        sh = 1 << j
        has_partner = np.stack([r8 >= sh, r8 < SUBLANES - sh]).astype(np.float32)
        steps.append(lanes(CH_T * sh) * has_partner[None, :, None, None, :, None])
    ast = jnp.stack(steps, axis=2)
    ast = ast.transpose(0, 1, 4, 2, 3, 5, 6)
    per_row = jnp.concatenate([lanes(CH_T * (r + 1)) for r in range(SUBLANES)], axis=4)
    pc = jnp.stack([per_row[:, 0], per_row[:, 1, :, :, ::-1]], axis=1)
    pc = pc.transpose(0, 1, 3, 2, 4, 5)
    return wc, vk, vc, ast, pc


def _gelu_tanh(x):
    c = math.sqrt(2.0 / math.pi)
    return 0.5 * x * (1.0 + jnp.tanh(c * (x + 0.044715 * (x * x * x))))


def _merge_kernel(h_ref, yf_ref, yb_ref, us_ref, at_ref, gs_ref, ga_ref, d_ref,
                  wglu_ref, wbs_ref, wba_ref, wout_ref, o_ref):
    lanes = lambda ref: jnp.concatenate([ref[s] for s in range(N_SLAB)], axis=1)
    y = lanes(yf_ref) + lanes(yb_ref) + d_ref[...] * lanes(us_ref)
    z = _gelu_tanh(y)
    gl = jnp.dot(z.astype(bf16), wglu_ref[...], preferred_element_type=f32)
    ys = (z * jax.nn.sigmoid(gl)).astype(bf16)
    m1 = jnp.dot(ys, wbs_ref[...], preferred_element_type=f32)
    m2 = jnp.dot(at_ref[...], wba_ref[...], preferred_element_type=f32)
    merged = (jax.nn.sigmoid(gs_ref[...].astype(f32)) * m1
              + jax.nn.sigmoid(ga_ref[...].astype(f32)) * m2)
    out = h_ref[...] + jnp.dot(merged.astype(bf16), wout_ref[...], preferred_element_type=f32)
    rows = pl.program_id(0) * TM + lax.broadcasted_iota(jnp.int32, (TM, 1), 0)
    o_ref[...] = jnp.where(rows >= PAD, out, 0.0)


def _merge(l, h, yf, yb, us, proj, attn, d, wglu, wbs, wba, wout):
    row = lambda w, col: pl.BlockSpec((TM, w), lambda i: (i, col))
    slab = pl.BlockSpec((N_SLAB, TM, LANES), lambda i: (0, i, 0))
    full = lambda r, c: _resident((None, r, c), lambda i: (l, 0, 0))
    return pl.pallas_call(
        _merge_kernel,
        grid=(LP // TM,),
        in_specs=[
            row(D_MODEL, 0), slab, slab, slab, row(ATTN_Q, 0),
            row(D_MODEL, COL_GS // D_MODEL), row(D_MODEL, COL_GA // D_MODEL),
            full(1, SSM_WIDTH), full(SSM_WIDTH, SSM_WIDTH), full(SSM_WIDTH, D_MODEL),
            full(ATTN_Q, D_MODEL), full(D_MODEL, D_MODEL),
        ],
        out_specs=row(D_MODEL, 0),
        out_shape=jax.ShapeDtypeStruct((LP, D_MODEL), f32),
        compiler_params=_cparams(("parallel",)),
        name="merge",
    )(h, yf, yb, us, attn, proj, proj, d, wglu, wbs, wba, wout)


def kernel(x, meta_tokens, ffn1_norm, ffn1_w_gate, ffn1_w_up, ffn1_w_down, mix_norm, w_in, ssm_lam_re, ssm_lam_im, ssm_log_dt, ssm_b_re, ssm_b_im, ssm_c_re, ssm_c_im, ssm_d, ssm_w_glu, attn_sink, w_branch_ssm, w_branch_attn, w_out, ffn2_norm, ffn2_w_gate, ffn2_w_up, ffn2_w_down, final_norm):
    assert x.shape == (1, SEQ, D_MODEL)
    h = jnp.concatenate([jnp.zeros((PAD, D_MODEL), f32), meta_tokens.astype(f32), x[0]], axis=0)

    col_scale = jnp.where(jnp.arange(IN_COLS) < ATTN_Q, LOG2E * HEAD_DIM ** -0.5, 1.0).astype(f32)
    w_in_b = (w_in * col_scale).astype(bf16)
    cast = lambda w: w.astype(bf16)
    f1g, f1u, f1d = cast(ffn1_w_gate), cast(ffn1_w_up), cast(ffn1_w_down)
    f2g, f2u, f2d = cast(ffn2_w_gate), cast(ffn2_w_up), cast(ffn2_w_down)
    wglu, wbs, wba, wo = cast(ssm_w_glu), cast(w_branch_ssm), cast(w_branch_attn), cast(w_out)
    ssm_mats = _ssm_params(ssm_lam_re, ssm_lam_im, ssm_log_dt, ssm_b_re, ssm_b_im, ssm_c_re, ssm_c_im)
    bias = _attn_bias()

    n1, nm, n2, dskip = (v[:, None, :] for v in (ffn1_norm, mix_norm, ffn2_norm, ssm_d))
    for l in range(DEPTH):
        h = _ffn(l, h, n1, f1g, f1u, f1d)
        proj, us = _inproj(l, h, nm, w_in_b)
        attn = _attn(proj, attn_sink[l], bias)
        yf = _ssm(l, 0, us, *ssm_mats)
        yb = _ssm(l, 1, us, *ssm_mats)
        h = _merge(l, h, yf, yb, us, proj, attn, dskip, wglu, wbs, wba, wo)
        if l + 1 < DEPTH:
            h = _ffn(l, h, n2, f2g, f2u, f2d)
    return _ffn_final(DEPTH - 1, h, n2, f2g, f2u, f2d, final_norm[None])[None]
```

```python
import functools
import math

import numpy as np
import jax
import jax.numpy as jnp
from jax import lax
from jax.experimental import pallas as pl
from jax.experimental.pallas import tpu as pltpu

f32 = jnp.float32
bf16 = jnp.bfloat16

D_MODEL = 1024
SEQ = 16384
DEPTH = 4
N_META = 16
N_HEADS = 16
N_KV_HEADS = 4
HEAD_DIM = 64
Q_GROUP = N_HEADS // N_KV_HEADS
WINDOW = 128
BLOCK = 128
SSM_WIDTH = 512
SSM_GROUP_CH = 16
SSM_GROUPS = 32
SSM_STATE = 64
D_FF = 2816
EPS = 1e-6
NEG = -1e30
LOG2E = math.log2(math.e)

ATTN_Q = N_HEADS * HEAD_DIM
ATTN_KV = N_KV_HEADS * HEAD_DIM
IN_COLS = 4096
COL_K = ATTN_Q
COL_V = ATTN_Q + ATTN_KV
COL_U = ATTN_Q + 2 * ATTN_KV
COL_GS = COL_U + SSM_WIDTH
COL_GA = COL_GS + D_MODEL

PAD = BLOCK - N_META
LP = PAD + N_META + SEQ
NB = LP // BLOCK

LANES = 128
SUBLANES = 8
VMEM_LIMIT = 56 * 1024 * 1024

TM = 688
TM_FFN = TM
MXU_TILE = 256
FF_CHUNK = 2 * MXU_TILE
TN_IN = 4 * MXU_TILE
T_FINAL = 512
HEADS_PER_DOT = 2
QB_STEP = 3

CH_T = SUBLANES
N_SLAB = SSM_WIDTH // LANES
G_SLAB = SSM_GROUPS // N_SLAB
SLAB_STATE = G_SLAB * SSM_STATE
CAT = CH_T * LANES
T_SSM = LP // 3
NT_SSM = LP // T_SSM
R_SSM = T_SSM // CH_T
R_VREG = R_SSM // SUBLANES
EXPAND_ROWS = 256


def _cparams(sem):
    return pltpu.CompilerParams(dimension_semantics=sem, vmem_limit_bytes=VMEM_LIMIT)


def _rmsnorm(x, g):
    ms = jnp.mean(x * x, axis=-1, keepdims=True)
    return x * lax.rsqrt(ms + EPS) * g


def _ffn_kernel(x_ref, g_ref, wg_ref, wu_ref, wd_ref, *rest):
    o_ref = rest[-1]
    x = x_ref[...]
    xn = _rmsnorm(x, g_ref[...]).astype(bf16)
    acc = None
    for c0 in range(0, D_FF, FF_CHUNK):
        c1 = min(c0 + FF_CHUNK, D_FF)
        a = jnp.dot(xn, wg_ref[:, c0:c1], preferred_element_type=f32)
        b = jnp.dot(xn, wu_ref[:, c0:c1], preferred_element_type=f32)
        mid = (a * jax.nn.sigmoid(a) * b).astype(bf16)
        part = jnp.dot(mid, wd_ref[c0:c1, :], preferred_element_type=f32)
        acc = part if acc is None else acc + part
    out = x + 0.5 * acc
    o_ref[...] = _rmsnorm(out, rest[0][...]) if len(rest) == 2 else out


def _resident(shape, index_map):
    return pl.BlockSpec(shape, index_map, pipeline_mode=pl.Buffered(1))


def _ffn(l, h, g, wg, wu, wd):
    return pl.pallas_call(
        _ffn_kernel,
        grid=(LP // TM_FFN,),
        in_specs=[
            pl.BlockSpec((TM_FFN, D_MODEL), lambda i: (i, 0)),
            _resident((None, 1, D_MODEL), lambda i: (l, 0, 0)),
            _resident((None, D_MODEL, D_FF), lambda i: (l, 0, 0)),
            _resident((None, D_MODEL, D_FF), lambda i: (l, 0, 0)),
            _resident((None, D_FF, D_MODEL), lambda i: (l, 0, 0)),
        ],
        out_specs=pl.BlockSpec((TM_FFN, D_MODEL), lambda i: (i, 0)),
        out_shape=jax.ShapeDtypeStruct((LP, D_MODEL), f32),
        compiler_params=_cparams(("parallel",)),
        name="ffn",
    )(h, g, wg, wu, wd)


def _ffn_final(l, h, g, wg, wu, wd, g_final):
    weights = lambda r, c: _resident((None, r, c), lambda i: (l, 0, 0))
    return pl.pallas_call(
        _ffn_kernel,
        grid=(SEQ // T_FINAL,),
        in_specs=[
            pl.BlockSpec((pl.Element(T_FINAL), pl.Element(D_MODEL)),
                         lambda i: (pl.multiple_of(BLOCK + i * T_FINAL, BLOCK), 0)),
            weights(1, D_MODEL), weights(D_MODEL, D_FF), weights(D_MODEL, D_FF), weights(D_FF, D_MODEL),
            _resident((1, D_MODEL), lambda i: (0, 0)),
        ],
        out_specs=pl.BlockSpec((T_FINAL, D_MODEL), lambda i: (i, 0)),
        out_shape=jax.ShapeDtypeStruct((SEQ, D_MODEL), f32),
        compiler_params=_cparams(("parallel",)),
        name="ffn_final",
    )(h, g, wg, wu, wd, g_final)


def _inproj_kernel(x_ref, g_ref, w_ref, o_ref, us_ref):
    xn = _rmsnorm(x_ref[...], g_ref[...]).astype(bf16)
    for c0 in range(0, IN_COLS, TN_IN):
        res = jnp.dot(xn, w_ref[:, c0:c0 + TN_IN], preferred_element_type=f32)
        o_ref[:, c0:c0 + TN_IN] = res.astype(bf16)
        for s in range(N_SLAB):
            u0 = COL_U + s * LANES
            if c0 <= u0 < c0 + TN_IN:
                us_ref[s] = res[:, u0 - c0:u0 - c0 + LANES]


def _inproj(l, h, g, w):
    return pl.pallas_call(
        _inproj_kernel,
        grid=(LP // TM,),
        in_specs=[
            pl.BlockSpec((TM, D_MODEL), lambda i: (i, 0)),
            _resident((None, 1, D_MODEL), lambda i: (l, 0, 0)),
            _resident((None, D_MODEL, IN_COLS), lambda i: (l, 0, 0)),
        ],
        out_specs=[pl.BlockSpec((TM, IN_COLS), lambda i: (i, 0)),
                   pl.BlockSpec((N_SLAB, TM, LANES), lambda i: (0, i, 0))],
        out_shape=[jax.ShapeDtypeStruct((LP, IN_COLS), bf16),
                   jax.ShapeDtypeStruct((N_SLAB, LP, LANES), f32)],
        compiler_params=_cparams(("parallel",)),
        name="inproj",
    )(h, g, w)


def _attn_block(sink_ref, q_ref, k_refs, v_refs, bias_ref, o_ref):
    k_all, v_all = [], []
    for kh in range(N_KV_HEADS):
        sl = slice(kh * HEAD_DIM, (kh + 1) * HEAD_DIM)
        k_all.append(jnp.concatenate([r[:, sl] for r in k_refs], axis=0))
        v_all.append(jnp.concatenate([r[:, sl] for r in v_refs], axis=0))

    def logits(t):
        heads = range(t * HEADS_PER_DOT, (t + 1) * HEADS_PER_DOT)
        q = jnp.concatenate([q_ref[:, h * HEAD_DIM:(h + 1) * HEAD_DIM] for h in heads], axis=0)
        return lax.dot_general(q, k_all[heads[0] // Q_GROUP], (((1,), (1,)), ((), ())),
                               preferred_element_type=f32)

    sink_lane = lax.broadcasted_iota(jnp.int32, (BLOCK, BLOCK), 1) == 0

    def softmax(t, s):
        ps, denoms = [], []
        for j in range(HEADS_PER_DOT):
            h = t * HEADS_PER_DOT + j
            sh = s[j * BLOCK:(j + 1) * BLOCK] + bias_ref[0, h]
            meta = jnp.where(sink_lane, sink_ref[h] * LOG2E, sh[:, 3 * BLOCK:])
            sh = jnp.concatenate([sh[:, :3 * BLOCK], meta], axis=1)
            p = jnp.exp2(sh - jnp.max(sh, axis=-1, keepdims=True))
            denoms.append(jnp.sum(p, axis=-1, keepdims=True))
            ps.append(p.astype(bf16))
        return jnp.concatenate(ps, axis=0), denoms

    def finish(t, p, denoms):
        o = jnp.dot(p, v_all[t * HEADS_PER_DOT // Q_GROUP], preferred_element_type=f32)
        outs = [o[j * BLOCK:(j + 1) * BLOCK] / denoms[j] for j in range(HEADS_PER_DOT)]
        w = HEADS_PER_DOT * HEAD_DIM
        o_ref[:, t * w:(t + 1) * w] = jnp.concatenate(outs, axis=1).astype(bf16)

    return logits, softmax, finish


def _attn_kernel(sink_ref, q_ref, *refs):
    n_kv = QB_STEP + 2
    k_refs, km_ref = refs[:n_kv], refs[n_kv]
    v_refs, vm_ref = refs[n_kv + 1:2 * n_kv + 1], refs[2 * n_kv + 1]
    bias_refs = refs[2 * n_kv + 2:2 * n_kv + 2 + QB_STEP]
    o_ref = refs[-1]
    blocks = []
    for i in range(QB_STEP):
        rows = slice(i * BLOCK, (i + 1) * BLOCK)
        blocks.append(_attn_block(sink_ref, q_ref.at[rows], list(k_refs[i:i + 3]) + [km_ref],
                                  list(v_refs[i:i + 3]) + [vm_ref], bias_refs[i], o_ref.at[rows]))

    per_block = N_HEADS // HEADS_PER_DOT
    work = [(blk, t) for blk in blocks for t in range(per_block)]
    logits = lambda n: work[n][0][0](work[n][1])
    softmax = lambda n, s: work[n][0][1](work[n][1], s)
    s_q = {0: logits(0), 1: logits(1)}
    p_q = {0: softmax(0, s_q.pop(0))}
    for n in range(len(work)):
        if n + 2 < len(work):
            s_q[n + 2] = logits(n + 2)
        if n + 1 < len(work):
            p_q[n + 1] = softmax(n + 1, s_q.pop(n + 1))
        work[n][0][2](work[n][1], *p_q.pop(n))


def _attn(proj, sink, bias):
    n_steps = NB // QB_STEP
    clamp = lambda b: jnp.clip(b, 0, NB - 1)
    kv = lambda col, off: pl.BlockSpec((BLOCK, ATTN_KV), lambda n: (clamp(n * QB_STEP + off), col))
    first = lambda col: pl.BlockSpec((BLOCK, ATTN_KV), lambda n: (0, col))
    ck, cv = COL_K // ATTN_KV, COL_V // ATTN_KV
    offs = range(-1, QB_STEP + 1)

    def bias_spec(i):
        def variant(n):
            b = n * QB_STEP + i
            return jnp.where(b == 0, 0, jnp.where(b == 1, 1, jnp.where(b == NB - 1, 3, 2)))
        return pl.BlockSpec((1, N_HEADS, BLOCK, 4 * BLOCK), lambda n: (variant(n), 0, 0, 0),
                            pipeline_mode=pl.Buffered(1))

    return pl.pallas_call(
        _attn_kernel,
        grid=(n_steps,),
        in_specs=[pl.BlockSpec(memory_space=pltpu.SMEM),
                  pl.BlockSpec((QB_STEP * BLOCK, ATTN_Q), lambda n: (n, 0))]
                 + [kv(ck, o) for o in offs] + [first(ck)]
                 + [kv(cv, o) for o in offs] + [first(cv)]
                 + [bias_spec(i) for i in range(QB_STEP)],
        out_specs=pl.BlockSpec((QB_STEP * BLOCK, ATTN_Q), lambda n: (n, 0)),
        out_shape=jax.ShapeDtypeStruct((LP, ATTN_Q), bf16),
        compiler_params=_cparams(("parallel",)),
        name="attn",
    )(sink, proj, *([proj] * (2 * (QB_STEP + 3))), *([bias] * QB_STEP))


def _attn_bias():
    qi = np.arange(BLOCK)[:, None]
    sj = np.arange(3 * BLOCK)[None, :]
    dist = np.abs(qi + BLOCK - sj)
    slopes = 2.0 ** (-8.0 * np.arange(1, N_HEADS + 1) / N_HEADS)
    band = -LOG2E * slopes[:, None, None] * dist[None].astype(np.float64)
    in_win = (dist <= WINDOW)[None]
    meta = np.where(np.arange(BLOCK) >= PAD, 0.0, NEG)[None, None, :]
    meta = np.broadcast_to(meta, (N_HEADS, BLOCK, BLOCK))
    out = []
    for blk_ok in ((False, False, True), (False, True, True), (True, True, True), (True, True, False)):
        kvalid = np.repeat(np.asarray(blk_ok), BLOCK)[None, None, :]
        b = np.where(in_win & kvalid, band, NEG)
        out.append(np.concatenate([b, meta], axis=-1))
    return jnp.asarray(np.stack(out), dtype=f32)


def _cmul(ar, ai, br, bi):
    return ar * br - ai * bi, ar * bi + ai * br


def _expand_block_diag(src_ref, dst_ref, row_inner, col_inner):
    lane = lax.broadcasted_iota(jnp.int32, (LANES, CAT), 0)
    col = lax.broadcasted_iota(jnp.int32, (LANES, CAT), 1)
    src_lane = (col // (G_SLAB * col_inner)) * col_inner + col % col_inner
    spread = jnp.where(src_lane == lane, 1.0, 0.0).astype(bf16)
    col_g = (lax.broadcasted_iota(jnp.int32, (1, CAT), 1) // col_inner) % G_SLAB
    for r0 in range(0, CAT, EXPAND_ROWS):
        row_g = ((r0 + lax.broadcasted_iota(jnp.int32, (EXPAND_ROWS, 1), 0)) // row_inner) % G_SLAB
        full = jnp.dot(src_ref[r0:r0 + EXPAND_ROWS, :].astype(bf16), spread, preferred_element_type=f32)
        dst_ref[r0:r0 + EXPAND_ROWS, :] = jnp.where(row_g == col_g, full, 0.0).astype(bf16)


def _ssm_kernel(reverse, u_ref, wc_ref, vk_ref, vc_ref, ast_ref, pc_ref, y_ref,
                w_ref, m_ref, v_ref, ucat_ref, loc_ref, sp_ref, ycat_ref, carry_ref):
    @pl.when(pl.program_id(1) == 0)
    def _():
        carry_ref[...] = jnp.zeros_like(carry_ref)
        _expand_block_diag(wc_ref, w_ref, SSM_GROUP_CH, SSM_STATE)
        _expand_block_diag(vk_ref, v_ref, SSM_STATE, SSM_GROUP_CH)
        s0 = (0 if reverse else CH_T - 1) * LANES
        strip = jnp.dot(w_ref[s0:s0 + LANES, :], v_ref[...], preferred_element_type=f32).astype(bf16)
        for s in range(CH_T):
            lo, hi = (0, (s + 1) * LANES) if reverse else (s * LANES, CAT)
            src = (CH_T - 1 - s) * LANES if reverse else 0
            if lo > 0:
                m_ref[s * LANES:(s + 1) * LANES, :lo] = jnp.zeros((LANES, lo), bf16)
            m_ref[s * LANES:(s + 1) * LANES, lo:hi] = strip[:, src:src + hi - lo]
            if hi < CAT:
                m_ref[s * LANES:(s + 1) * LANES, hi:] = jnp.zeros((LANES, CAT - hi), bf16)
        _expand_block_diag(vc_ref, v_ref, SSM_STATE, SSM_GROUP_CH)

    def rows8(k):
        return pl.ds(pl.multiple_of(k * SUBLANES, SUBLANES), SUBLANES)

    def token_rows(k, t):
        return pl.ds(k * (SUBLANES * CH_T) + t, SUBLANES, stride=CH_T)

    def gather(k, _):
        for t in range(CH_T):
            ucat_ref[rows8(k), t * LANES:(t + 1) * LANES] = u_ref[token_rows(k, t), :]
        return 0

    lax.fori_loop(0, R_VREG, gather, 0, unroll=True)
    ucat = ucat_ref[...].astype(bf16)
    loc_ref[...] = jnp.dot(ucat, w_ref[...], preferred_element_type=f32)
    for c0 in range(0, CAT, MXU_TILE):
        k = slice(c0, CAT) if reverse else slice(0, c0 + MXU_TILE)
        ycat_ref[:, c0:c0 + MXU_TILE] = jnp.dot(ucat[:, k], m_ref[k, c0:c0 + MXU_TILE],
                                                preferred_element_type=f32)

    row = lax.broadcasted_iota(jnp.int32, (SUBLANES, SLAB_STATE), 0)
    edge = SUBLANES - 1 if reverse else 0
    last = 0 if reverse else SUBLANES - 1
    toward = lambda x, n: pltpu.roll(x, (SUBLANES - n) if reverse else n, 0)

    def scan(n, carry):
        k = R_VREG - 1 - n if reverse else n
        xr = loc_ref[rows8(k), :SLAB_STATE]
        xi = loc_ref[rows8(k), SLAB_STATE:]
        for j in range(3):
            tr, ti = _cmul(ast_ref[j, 0], ast_ref[j, 1], toward(xr, 1 << j), toward(xi, 1 << j))
            xr, xi = xr + tr, xi + ti
        cr, ci = carry
        tr, ti = _cmul(pc_ref[0], pc_ref[1], cr, ci)
        xr, xi = xr + tr, xi + ti
        sp_ref[rows8(k), :SLAB_STATE] = jnp.where(row == edge, cr, toward(xr, 1))
        sp_ref[rows8(k), SLAB_STATE:] = jnp.where(row == edge, ci, toward(xi, 1))
        bcast = lambda x: jnp.broadcast_to(x[last:last + 1], (SUBLANES, SLAB_STATE))
        return bcast(xr), bcast(xi)

    cr, ci = lax.fori_loop(0, R_VREG, scan, (carry_ref[0], carry_ref[1]), unroll=True)
    carry_ref[0] = cr
    carry_ref[1] = ci
    ycat_ref[...] += jnp.dot(sp_ref[...].astype(bf16), v_ref[...], preferred_element_type=f32)

    def scatter(k, _):
        for t in range(CH_T):
            y_ref[token_rows(k, t), :] = ycat_ref[rows8(k), t * LANES:(t + 1) * LANES]
        return 0

    lax.fori_loop(0, R_VREG, scatter, 0, unroll=True)


def _ssm(l, d, us, wc, vk, vc, ast, pc):
    tile = (lambda i: NT_SSM - 1 - i) if d else (lambda i: i)
    rows = pl.BlockSpec((None, T_SSM, LANES), lambda s, i: (s, tile(i), 0))
    mat = pl.BlockSpec((None, None, None, CAT, LANES), lambda s, i: (l, d, s, 0, 0))
    return pl.pallas_call(
        functools.partial(_ssm_kernel, bool(d)),
        grid=(N_SLAB, NT_SSM),
        in_specs=[
            rows, mat, mat, mat,
            pl.BlockSpec((None, None, None, 3, 2, SUBLANES, SLAB_STATE), lambda s, i: (l, d, s, 0, 0, 0, 0)),
            pl.BlockSpec((None, None, None, 2, SUBLANES, SLAB_STATE), lambda s, i: (l, d, s, 0, 0, 0)),
        ],
        out_specs=rows,
        out_shape=jax.ShapeDtypeStruct((N_SLAB, LP, LANES), f32),
        scratch_shapes=[
            pltpu.VMEM((CAT, CAT), bf16),
            pltpu.VMEM((CAT, CAT), bf16),
            pltpu.VMEM((CAT, CAT), bf16),
            pltpu.VMEM((R_SSM, CAT), f32),
            pltpu.VMEM((R_SSM, CAT), f32),
            pltpu.VMEM((R_SSM, CAT), f32),
            pltpu.VMEM((R_SSM, CAT), f32),
            pltpu.VMEM((2, SUBLANES, SLAB_STATE), f32),
        ],
        compiler_params=_cparams(("parallel", "arbitrary")),
        name="ssm",
    )(us, wc, vk, vc, ast, pc)


def _ssm_params(lam_re, lam_im, log_dt, b_re, b_im, c_re, c_im):
    dt = jnp.exp(log_dt)[..., None]

    def apow(n):
        mag = jnp.exp(lam_re * dt * n)
        return mag * jnp.cos(lam_im * dt * n), mag * jnp.sin(lam_im * dt * n)

    pows = [apow(n) for n in range(CH_T + 1)]
    forward = (jnp.arange(2) == 0)[None, :, None, None]

    def pick(n_fwd, n_bwd):
        return (jnp.where(forward, pows[n_fwd][0], pows[n_bwd][0]),
                jnp.where(forward, pows[n_fwd][1], pows[n_bwd][1]))

    ar, ai = pows[1]
    den = lam_re * lam_re + lam_im * lam_im
    fr = ((ar - 1.0) * lam_re + ai * lam_im) / den
    fi = (ai * lam_re - (ar - 1.0) * lam_im) / den
    bt_re, bt_im = jnp.swapaxes(b_re, -1, -2), jnp.swapaxes(b_im, -1, -2)
    bbr = fr[..., None, :] * bt_re - fi[..., None, :] * bt_im
    bbi = fr[..., None, :] * bt_im + fi[..., None, :] * bt_re
    ct_re, ct_im = jnp.swapaxes(c_re, -1, -2), jnp.swapaxes(c_im, -1, -2)
    slab = lambda x: x.reshape((DEPTH, 2, N_SLAB, G_SLAB) + x.shape[3:])

    rows = []
    for s in range(CH_T):
        pr, pi = pick(CH_T - 1 - s, s)
        pr, pi = pr[..., None, :], pi[..., None, :]
        rows.append(slab(jnp.concatenate([pr * bbr - pi * bbi, pr * bbi + pi * bbr], axis=-1)))
    wc = jnp.stack(rows, axis=3).reshape(DEPTH, 2, N_SLAB, CAT, LANES)

    def c_times_powers(n_fwd, n_bwd):
        re, im = [], []
        for j in range(CH_T):
            pr, pi = pick(n_fwd[j], n_bwd[j])
            pr, pi = pr[..., None], pi[..., None]
            re.append(ct_re * pr - ct_im * pi)
            im.append(-(ct_re * pi + ct_im * pr))
        parts = [slab(jnp.concatenate(x, axis=-1)) for x in (re, im)]
        return jnp.stack(parts, axis=3).reshape(DEPTH, 2, N_SLAB, CAT, LANES)

    steps = list(range(CH_T))
    vk = c_times_powers(steps, steps[::-1])
    vc = c_times_powers([t + 1 for t in steps], [CH_T - t for t in steps])

    def lanes(n):
        r, i = apow(n)
        return jnp.stack([r, i], axis=2).reshape(DEPTH, 2, 2, N_SLAB, 1, SLAB_STATE)

    r8 = np.arange(SUBLANES)
    steps = []
    for j in range(3):
        sh = 1 << j
        has_partner = np.stack([r8 >= sh, r8 < SUBLANES - sh]).astype(np.float32)
        steps.append(lanes(CH_T * sh) * has_partner[None, :, None, None, :, None])
    ast = jnp.stack(steps, axis=2)
    ast = ast.transpose(0, 1, 4, 2, 3, 5, 6)
    per_row = jnp.concatenate([lanes(CH_T * (r + 1)) for r in range(SUBLANES)], axis=4)
    pc = jnp.stack([per_row[:, 0], per_row[:, 1, :, :, ::-1]], axis=1)
    pc = pc.transpose(0, 1, 3, 2, 4, 5)
    return wc, vk, vc, ast, pc


def _gelu_tanh(x):
    c = math.sqrt(2.0 / math.pi)
    return 0.5 * x * (1.0 + jnp.tanh(c * (x + 0.044715 * (x * x * x))))


def _merge_kernel(h_ref, yf_ref, yb_ref, us_ref, at_ref, gs_ref, ga_ref, d_ref,
                  wglu_ref, wbs_ref, wba_ref, wout_ref, o_ref):
    lanes = lambda ref: jnp.concatenate([ref[s] for s in range(N_SLAB)], axis=1)
    y = lanes(yf_ref) + lanes(yb_ref) + d_ref[...] * lanes(us_ref)
    z = _gelu_tanh(y)
    gl = jnp.dot(z.astype(bf16), wglu_ref[...], preferred_element_type=f32)
    ys = (z * jax.nn.sigmoid(gl)).astype(bf16)
    m1 = jnp.dot(ys, wbs_ref[...], preferred_element_type=f32)
    m2 = jnp.dot(at_ref[...], wba_ref[...], preferred_element_type=f32)
    merged = (jax.nn.sigmoid(gs_ref[...].astype(f32)) * m1
              + jax.nn.sigmoid(ga_ref[...].astype(f32)) * m2)
    out = h_ref[...] + jnp.dot(merged.astype(bf16), wout_ref[...], preferred_element_type=f32)
    rows = pl.program_id(0) * TM + lax.broadcasted_iota(jnp.int32, (TM, 1), 0)
    o_ref[...] = jnp.where(rows >= PAD, out, 0.0)


def _merge(l, h, yf, yb, us, proj, attn, d, wglu, wbs, wba, wout):
    row = lambda w, col: pl.BlockSpec((TM, w), lambda i: (i, col))
    slab = pl.BlockSpec((N_SLAB, TM, LANES), lambda i: (0, i, 0))
    full = lambda r, c: _resident((None, r, c), lambda i: (l, 0, 0))
    return pl.pallas_call(
        _merge_kernel,
        grid=(LP // TM,),
        in_specs=[
            row(D_MODEL, 0), slab, slab, slab, row(ATTN_Q, 0),
            row(D_MODEL, COL_GS // D_MODEL), row(D_MODEL, COL_GA // D_MODEL),
            full(1, SSM_WIDTH), full(SSM_WIDTH, SSM_WIDTH), full(SSM_WIDTH, D_MODEL),
            full(ATTN_Q, D_MODEL), full(D_MODEL, D_MODEL),
        ],
        out_specs=row(D_MODEL, 0),
        out_shape=jax.ShapeDtypeStruct((LP, D_MODEL), f32),
        compiler_params=_cparams(("parallel",)),
        name="merge",
    )(h, yf, yb, us, attn, proj, proj, d, wglu, wbs, wba, wout)


def kernel(x, meta_tokens, ffn1_norm, ffn1_w_gate, ffn1_w_up, ffn1_w_down, mix_norm, w_in, ssm_lam_re, ssm_lam_im, ssm_log_dt, ssm_b_re, ssm_b_im, ssm_c_re, ssm_c_im, ssm_d, ssm_w_glu, attn_sink, w_branch_ssm, w_branch_attn, w_out, ffn2_norm, ffn2_w_gate, ffn2_w_up, ffn2_w_down, final_norm):
    assert x.shape == (1, SEQ, D_MODEL)
    h = jnp.concatenate([jnp.zeros((PAD, D_MODEL), f32), meta_tokens.astype(f32), x[0]], axis=0)

    col_scale = jnp.where(jnp.arange(IN_COLS) < ATTN_Q, LOG2E * HEAD_DIM ** -0.5, 1.0).astype(f32)
    w_in_b = (w_in * col_scale).astype(bf16)
    cast = lambda w: w.astype(bf16)
    f1g, f1u, f1d = cast(ffn1_w_gate), cast(ffn1_w_up), cast(ffn1_w_down)
    f2g, f2u, f2d = cast(ffn2_w_gate), cast(ffn2_w_up), cast(ffn2_w_down)
    wglu, wbs, wba, wo = cast(ssm_w_glu), cast(w_branch_ssm), cast(w_branch_attn), cast(w_out)
    ssm_mats = _ssm_params(ssm_lam_re, ssm_lam_im, ssm_log_dt, ssm_b_re, ssm_b_im, ssm_c_re, ssm_c_im)
    bias = _attn_bias()

    n1, nm, n2, dskip = (v[:, None, :] for v in (ffn1_norm, mix_norm, ffn2_norm, ssm_d))
    for l in range(DEPTH):
        h = _ffn(l, h, n1, f1g, f1u, f1d)
        proj, us = _inproj(l, h, nm, w_in_b)
        attn = _attn(proj, attn_sink[l], bias)
        yf = _ssm(l, 0, us, *ssm_mats)
        yb = _ssm(l, 1, us, *ssm_mats)
        h = _merge(l, h, yf, yb, us, proj, attn, dskip, wglu, wbs, wba, wo)
        if l + 1 < DEPTH:
            h = _ffn(l, h, n2, f2g, f2u, f2d)
    return _ffn_final(DEPTH - 1, h, n2, f2g, f2u, f2d, final_norm[None])[None]
```

```python
import functools
import math

import numpy as np
import jax
import jax.numpy as jnp
from jax import lax
from jax.experimental import pallas as pl
from jax.experimental.pallas import tpu as pltpu

f32 = jnp.float32
bf16 = jnp.bfloat16

D_MODEL = 1024
SEQ = 16384
DEPTH = 4
N_META = 16
N_HEADS = 16
N_KV_HEADS = 4
HEAD_DIM = 64
Q_GROUP = N_HEADS // N_KV_HEADS
WINDOW = 128
BLOCK = 128
SSM_WIDTH = 512
SSM_GROUP_CH = 16
SSM_GROUPS = 32
SSM_STATE = 64
D_FF = 2816
EPS = 1e-6
NEG = -1e30
LOG2E = math.log2(math.e)

ATTN_Q = N_HEADS * HEAD_DIM
ATTN_KV = N_KV_HEADS * HEAD_DIM
IN_COLS = 4096
COL_K = ATTN_Q
COL_V = ATTN_Q + ATTN_KV
COL_U = ATTN_Q + 2 * ATTN_KV
COL_GS = COL_U + SSM_WIDTH
COL_GA = COL_GS + D_MODEL

PAD = BLOCK - N_META
LP = PAD + N_META + SEQ
NB = LP // BLOCK

LANES = 128
SUBLANES = 8
VMEM_LIMIT = 56 * 1024 * 1024

TM = 688
TM_FFN = TM
MXU_TILE = 256
FF_CHUNK = 2 * MXU_TILE
TN_IN = 4 * MXU_TILE
T_FINAL = 512
HEADS_PER_DOT = 2
QB_STEP = 3

CH_T = SUBLANES
N_SLAB = SSM_WIDTH // LANES
G_SLAB = SSM_GROUPS // N_SLAB
SLAB_STATE = G_SLAB * SSM_STATE
CAT = CH_T * LANES
T_SSM = LP // 3
NT_SSM = LP // T_SSM
R_SSM = T_SSM // CH_T
R_VREG = R_SSM // SUBLANES
EXPAND_ROWS = 256


def _cparams(sem):
    return pltpu.CompilerParams(dimension_semantics=sem, vmem_limit_bytes=VMEM_LIMIT)


def _rmsnorm(x, g):
    ms = jnp.mean(x * x, axis=-1, keepdims=True)
    return x * lax.rsqrt(ms + EPS) * g


def _ffn_kernel(x_ref, g_ref, wg_ref, wu_ref, wd_ref, *rest):
    o_ref = rest[-1]
    x = x_ref[...]
    xn = _rmsnorm(x, g_ref[...]).astype(bf16)
    acc = None
    for c0 in range(0, D_FF, FF_CHUNK):
        c1 = min(c0 + FF_CHUNK, D_FF)
        a = jnp.dot(xn, wg_ref[:, c0:c1], preferred_element_type=f32)
        b = jnp.dot(xn, wu_ref[:, c0:c1], preferred_element_type=f32)
        mid = (a * jax.nn.sigmoid(a) * b).astype(bf16)
        part = jnp.dot(mid, wd_ref[c0:c1, :], preferred_element_type=f32)
        acc = part if acc is None else acc + part
    out = x + 0.5 * acc
    o_ref[...] = _rmsnorm(out, rest[0][...]) if len(rest) == 2 else out


def _resident(shape, index_map):
    return pl.BlockSpec(shape, index_map, pipeline_mode=pl.Buffered(1))


def _ffn(l, h, g, wg, wu, wd):
    return pl.pallas_call(
        _ffn_kernel,
        grid=(LP // TM_FFN,),
        in_specs=[
            pl.BlockSpec((TM_FFN, D_MODEL), lambda i: (i, 0)),
            _resident((None, 1, D_MODEL), lambda i: (l, 0, 0)),
            _resident((None, D_MODEL, D_FF), lambda i: (l, 0, 0)),
            _resident((None, D_MODEL, D_FF), lambda i: (l, 0, 0)),
            _resident((None, D_FF, D_MODEL), lambda i: (l, 0, 0)),
        ],
        out_specs=pl.BlockSpec((TM_FFN, D_MODEL), lambda i: (i, 0)),
        out_shape=jax.ShapeDtypeStruct((LP, D_MODEL), f32),
        compiler_params=_cparams(("parallel",)),
        name="ffn",
    )(h, g, wg, wu, wd)


def _ffn_final(l, h, g, wg, wu, wd, g_final):
    weights = lambda r, c: _resident((None, r, c), lambda i: (l, 0, 0))
    return pl.pallas_call(
        _ffn_kernel,
        grid=(SEQ // T_FINAL,),
        in_specs=[
            pl.BlockSpec((pl.Element(T_FINAL), pl.Element(D_MODEL)),
                         lambda i: (pl.multiple_of(BLOCK + i * T_FINAL, BLOCK), 0)),
            weights(1, D_MODEL), weights(D_MODEL, D_FF), weights(D_MODEL, D_FF), weights(D_FF, D_MODEL),
            _resident((1, D_MODEL), lambda i: (0, 0)),
        ],
        out_specs=pl.BlockSpec((T_FINAL, D_MODEL), lambda i: (i, 0)),
        out_shape=jax.ShapeDtypeStruct((SEQ, D_MODEL), f32),
        compiler_params=_cparams(("parallel",)),
        name="ffn_final",
    )(h, g, wg, wu, wd, g_final)


def _inproj_kernel(x_ref, g_ref, w_ref, o_ref, us_ref):
    xn = _rmsnorm(x_ref[...], g_ref[...]).astype(bf16)
    for c0 in range(0, IN_COLS, TN_IN):
        res = jnp.dot(xn, w_ref[:, c0:c0 + TN_IN], preferred_element_type=f32)
        o_ref[:, c0:c0 + TN_IN] = res.astype(bf16)
        for s in range(N_SLAB):
            u0 = COL_U + s * LANES
            if c0 <= u0 < c0 + TN_IN:
                us_ref[s] = res[:, u0 - c0:u0 - c0 + LANES]


def _inproj(l, h, g, w):
    return pl.pallas_call(
        _inproj_kernel,
        grid=(LP // TM,),
        in_specs=[
            pl.BlockSpec((TM, D_MODEL), lambda i: (i, 0)),
            _resident((None, 1, D_MODEL), lambda i: (l, 0, 0)),
            _resident((None, D_MODEL, IN_COLS), lambda i: (l, 0, 0)),
        ],
        out_specs=[pl.BlockSpec((TM, IN_COLS), lambda i: (i, 0)),
                   pl.BlockSpec((N_SLAB, TM, LANES), lambda i: (0, i, 0))],
        out_shape=[jax.ShapeDtypeStruct((LP, IN_COLS), bf16),
                   jax.ShapeDtypeStruct((N_SLAB, LP, LANES), f32)],
        compiler_params=_cparams(("parallel",)),
        name="inproj",
    )(h, g, w)


def _attn_block(sink_ref, q_ref, k_refs, v_refs, bias_ref, o_ref):
    def kv_rows(refs, kh):
        sl = slice(kh * HEAD_DIM, (kh + 1) * HEAD_DIM)
        return jnp.concatenate([r[:, sl] for r in refs], axis=0)

    def logits(t):
        heads = range(t * HEADS_PER_DOT, (t + 1) * HEADS_PER_DOT)
        q = jnp.concatenate([q_ref[:, h * HEAD_DIM:(h + 1) * HEAD_DIM] for h in heads], axis=0)
        return lax.dot_general(q, kv_rows(k_refs, heads[0] // Q_GROUP), (((1,), (1,)), ((), ())),
                               preferred_element_type=f32)

    sink_lane = lax.broadcasted_iota(jnp.int32, (BLOCK, BLOCK), 1) == 0

    def softmax(t, s):
        ps, denoms = [], []
        for j in range(HEADS_PER_DOT):
            h = t * HEADS_PER_DOT + j
            sh = s[j * BLOCK:(j + 1) * BLOCK] + bias_ref[0, h]
            meta = jnp.where(sink_lane, sink_ref[h] * LOG2E, sh[:, 3 * BLOCK:])
            sh = jnp.concatenate([sh[:, :3 * BLOCK], meta], axis=1)
            p = jnp.exp2(sh - jnp.max(sh, axis=-1, keepdims=True))
            denoms.append(jnp.sum(p, axis=-1, keepdims=True))
            ps.append(p.astype(bf16))
        return jnp.concatenate(ps, axis=0), denoms

    def finish(t, p, denoms):
        o = jnp.dot(p, kv_rows(v_refs, t * HEADS_PER_DOT // Q_GROUP), preferred_element_type=f32)
        outs = [o[j * BLOCK:(j + 1) * BLOCK] / denoms[j] for j in range(HEADS_PER_DOT)]
        w = HEADS_PER_DOT * HEAD_DIM
        o_ref[:, t * w:(t + 1) * w] = jnp.concatenate(outs, axis=1).astype(bf16)

    return logits, softmax, finish


def _attn_kernel(sink_ref, q_ref, *refs):
    n_kv = QB_STEP + 2
    k_refs, km_ref = refs[:n_kv], refs[n_kv]
    v_refs, vm_ref = refs[n_kv + 1:2 * n_kv + 1], refs[2 * n_kv + 1]
    bias_refs = refs[2 * n_kv + 2:2 * n_kv + 2 + QB_STEP]
    o_ref = refs[-1]
    blocks = []
    for i in range(QB_STEP):
        rows = slice(i * BLOCK, (i + 1) * BLOCK)
        blocks.append(_attn_block(sink_ref, q_ref.at[rows], list(k_refs[i:i + 3]) + [km_ref],
                                  list(v_refs[i:i + 3]) + [vm_ref], bias_refs[i], o_ref.at[rows]))

    per_block = N_HEADS // HEADS_PER_DOT
    work = [(blk, t) for blk in blocks for t in range(per_block)]
    logits = lambda n: work[n][0][0](work[n][1])
    softmax = lambda n, s: work[n][0][1](work[n][1], s)
    s_q = {0: logits(0), 1: logits(1)}
    p_q = {0: softmax(0, s_q.pop(0))}
    for n in range(len(work)):
        if n + 2 < len(work):
            s_q[n + 2] = logits(n + 2)
        if n + 1 < len(work):
            p_q[n + 1] = softmax(n + 1, s_q.pop(n + 1))
        work[n][0][2](work[n][1], *p_q.pop(n))


def _attn(proj, sink, bias):
    n_steps = NB // QB_STEP
    clamp = lambda b: jnp.clip(b, 0, NB - 1)
    kv = lambda col, off: pl.BlockSpec((BLOCK, ATTN_KV), lambda n: (clamp(n * QB_STEP + off), col))
    first = lambda col: pl.BlockSpec((BLOCK, ATTN_KV), lambda n: (0, col))
    ck, cv = COL_K // ATTN_KV, COL_V // ATTN_KV
    offs = range(-1, QB_STEP + 1)

    def bias_spec(i):
        def variant(n):
            b = n * QB_STEP + i
            return jnp.where(b == 0, 0, jnp.where(b == 1, 1, jnp.where(b == NB - 1, 3, 2)))
        return pl.BlockSpec((1, N_HEADS, BLOCK, 4 * BLOCK), lambda n: (variant(n), 0, 0, 0),
                            pipeline_mode=pl.Buffered(1))

    return pl.pallas_call(
        _attn_kernel,
        grid=(n_steps,),
        in_specs=[pl.BlockSpec(memory_space=pltpu.SMEM),
                  pl.BlockSpec((QB_STEP * BLOCK, ATTN_Q), lambda n: (n, 0))]
                 + [kv(ck, o) for o in offs] + [first(ck)]
                 + [kv(cv, o) for o in offs] + [first(cv)]
                 + [bias_spec(i) for i in range(QB_STEP)],
        out_specs=pl.BlockSpec((QB_STEP * BLOCK, ATTN_Q), lambda n: (n, 0)),
        out_shape=jax.ShapeDtypeStruct((LP, ATTN_Q), bf16),
        compiler_params=_cparams(("parallel",)),
        name="attn",
    )(sink, proj, *([proj] * (2 * (QB_STEP + 3))), *([bias] * QB_STEP))


def _attn_bias():
    qi = np.arange(BLOCK)[:, None]
    sj = np.arange(3 * BLOCK)[None, :]
    dist = np.abs(qi + BLOCK - sj)
    slopes = 2.0 ** (-8.0 * np.arange(1, N_HEADS + 1) / N_HEADS)
    band = -LOG2E * slopes[:, None, None] * dist[None].astype(np.float64)
    in_win = (dist <= WINDOW)[None]
    meta = np.where(np.arange(BLOCK) >= PAD, 0.0, NEG)[None, None, :]
    meta = np.broadcast_to(meta, (N_HEADS, BLOCK, BLOCK))
    out = []
    for blk_ok in ((False, False, True), (False, True, True), (True, True, True), (True, True, False)):
        kvalid = np.repeat(np.asarray(blk_ok), BLOCK)[None, None, :]
        b = np.where(in_win & kvalid, band, NEG)
        out.append(np.concatenate([b, meta], axis=-1))
    return jnp.asarray(np.stack(out), dtype=f32)


def _cmul(ar, ai, br, bi):
    return ar * br - ai * bi, ar * bi + ai * br


def _expand_block_diag(src_ref, dst_ref, row_inner, col_inner):
    lane = lax.broadcasted_iota(jnp.int32, (LANES, CAT), 0)
    col = lax.broadcasted_iota(jnp.int32, (LANES, CAT), 1)
    src_lane = (col // (G_SLAB * col_inner)) * col_inner + col % col_inner
    spread = jnp.where(src_lane == lane, 1.0, 0.0).astype(bf16)
    col_g = (lax.broadcasted_iota(jnp.int32, (1, CAT), 1) // col_inner) % G_SLAB
    for r0 in range(0, CAT, EXPAND_ROWS):
        row_g = ((r0 + lax.broadcasted_iota(jnp.int32, (EXPAND_ROWS, 1), 0)) // row_inner) % G_SLAB
        full = jnp.dot(src_ref[r0:r0 + EXPAND_ROWS, :].astype(bf16), spread, preferred_element_type=f32)
        dst_ref[r0:r0 + EXPAND_ROWS, :] = jnp.where(row_g == col_g, full, 0.0).astype(bf16)


def _ssm_kernel(reverse, u_ref, wc_ref, vk_ref, vc_ref, ast_ref, pc_ref, y_ref,
                w_ref, m_ref, v_ref, ucat_ref, loc_ref, sp_ref, ycat_ref, carry_ref):
    @pl.when(pl.program_id(1) == 0)
    def _():
        carry_ref[...] = jnp.zeros_like(carry_ref)
        _expand_block_diag(wc_ref, w_ref, SSM_GROUP_CH, SSM_STATE)
        _expand_block_diag(vk_ref, v_ref, SSM_STATE, SSM_GROUP_CH)
        s0 = (0 if reverse else CH_T - 1) * LANES
        strip = jnp.dot(w_ref[s0:s0 + LANES, :], v_ref[...], preferred_element_type=f32).astype(bf16)
        for s in range(CH_T):
            lo, hi = (0, (s + 1) * LANES) if reverse else (s * LANES, CAT)
            src = (CH_T - 1 - s) * LANES if reverse else 0
            if lo > 0:
                m_ref[s * LANES:(s + 1) * LANES, :lo] = jnp.zeros((LANES, lo), bf16)
            m_ref[s * LANES:(s + 1) * LANES, lo:hi] = strip[:, src:src + hi - lo]
            if hi < CAT:
                m_ref[s * LANES:(s + 1) * LANES, hi:] = jnp.zeros((LANES, CAT - hi), bf16)
        _expand_block_diag(vc_ref, v_ref, SSM_STATE, SSM_GROUP_CH)

    def rows8(k):
        return pl.ds(pl.multiple_of(k * SUBLANES, SUBLANES), SUBLANES)

    def token_rows(k, t):
        return pl.ds(k * (SUBLANES * CH_T) + t, SUBLANES, stride=CH_T)

    def gather(k, _):
        for t in range(CH_T):
            ucat_ref[rows8(k), t * LANES:(t + 1) * LANES] = u_ref[token_rows(k, t), :]
        return 0

    lax.fori_loop(0, R_VREG, gather, 0, unroll=True)
    ucat = ucat_ref[...].astype(bf16)
    loc_ref[...] = jnp.dot(ucat, w_ref[...], preferred_element_type=f32)
    for c0 in range(0, CAT, MXU_TILE):
        k = slice(c0, CAT) if reverse else slice(0, c0 + MXU_TILE)
        ycat_ref[:, c0:c0 + MXU_TILE] = jnp.dot(ucat[:, k], m_ref[k, c0:c0 + MXU_TILE],
                                                preferred_element_type=f32)

    row = lax.broadcasted_iota(jnp.int32, (SUBLANES, SLAB_STATE), 0)
    edge = SUBLANES - 1 if reverse else 0
    last = 0 if reverse else SUBLANES - 1
    toward = lambda x, n: pltpu.roll(x, (SUBLANES - n) if reverse else n, 0)

    def scan(n, carry):
        k = R_VREG - 1 - n if reverse else n
        xr = loc_ref[rows8(k), :SLAB_STATE]
        xi = loc_ref[rows8(k), SLAB_STATE:]
        for j in range(3):
            tr, ti = _cmul(ast_ref[j, 0], ast_ref[j, 1], toward(xr, 1 << j), toward(xi, 1 << j))
            xr, xi = xr + tr, xi + ti
        cr, ci = carry
        tr, ti = _cmul(pc_ref[0], pc_ref[1], cr, ci)
        xr, xi = xr + tr, xi + ti
        sp_ref[rows8(k), :SLAB_STATE] = jnp.where(row == edge, cr, toward(xr, 1))
        sp_ref[rows8(k), SLAB_STATE:] = jnp.where(row == edge, ci, toward(xi, 1))
        bcast = lambda x: jnp.broadcast_to(x[last:last + 1], (SUBLANES, SLAB_STATE))
        return bcast(xr), bcast(xi)

    cr, ci = lax.fori_loop(0, R_VREG, scan, (carry_ref[0], carry_ref[1]), unroll=True)
    carry_ref[0] = cr
    carry_ref[1] = ci
    ycat_ref[...] += jnp.dot(sp_ref[...].astype(bf16), v_ref[...], preferred_element_type=f32)

    def scatter(k, _):
        for t in range(CH_T):
            y_ref[token_rows(k, t), :] = ycat_ref[rows8(k), t * LANES:(t + 1) * LANES]
        return 0

    lax.fori_loop(0, R_VREG, scatter, 0, unroll=True)


def _ssm(l, d, us, wc, vk, vc, ast, pc):
    tile = (lambda i: NT_SSM - 1 - i) if d else (lambda i: i)
    rows = pl.BlockSpec((None, T_SSM, LANES), lambda s, i: (s, tile(i), 0))
    mat = pl.BlockSpec((None, None, None, CAT, LANES), lambda s, i: (l, d, s, 0, 0))
    return pl.pallas_call(
        functools.partial(_ssm_kernel, bool(d)),
        grid=(N_SLAB, NT_SSM),
        in_specs=[
            rows, mat, mat, mat,
            pl.BlockSpec((None, None, None, 3, 2, SUBLANES, SLAB_STATE), lambda s, i: (l, d, s, 0, 0, 0, 0)),
            pl.BlockSpec((None, None, None, 2, SUBLANES, SLAB_STATE), lambda s, i: (l, d, s, 0, 0, 0)),
        ],
        out_specs=rows,
        out_shape=jax.ShapeDtypeStruct((N_SLAB, LP, LANES), f32),
        scratch_shapes=[
            pltpu.VMEM((CAT, CAT), bf16),
            pltpu.VMEM((CAT, CAT), bf16),
            pltpu.VMEM((CAT, CAT), bf16),
            pltpu.VMEM((R_SSM, CAT), f32),
            pltpu.VMEM((R_SSM, CAT), f32),
            pltpu.VMEM((R_SSM, CAT), f32),
            pltpu.VMEM((R_SSM, CAT), f32),
            pltpu.VMEM((2, SUBLANES, SLAB_STATE), f32),
        ],
        compiler_params=_cparams(("parallel", "arbitrary")),
        name="ssm",
    )(us, wc, vk, vc, ast, pc)


def _ssm_params(lam_re, lam_im, log_dt, b_re, b_im, c_re, c_im):
    dt = jnp.exp(log_dt)[..., None]

    def apow(n):
        mag = jnp.exp(lam_re * dt * n)
        return mag * jnp.cos(lam_im * dt * n), mag * jnp.sin(lam_im * dt * n)

    pows = [apow(n) for n in range(CH_T + 1)]
    forward = (jnp.arange(2) == 0)[None, :, None, None]

    def pick(n_fwd, n_bwd):
        return (jnp.where(forward, pows[n_fwd][0], pows[n_bwd][0]),
                jnp.where(forward, pows[n_fwd][1], pows[n_bwd][1]))

    ar, ai = pows[1]
    den = lam_re * lam_re + lam_im * lam_im
    fr = ((ar - 1.0) * lam_re + ai * lam_im) / den
    fi = (ai * lam_re - (ar - 1.0) * lam_im) / den
    bt_re, bt_im = jnp.swapaxes(b_re, -1, -2), jnp.swapaxes(b_im, -1, -2)
    bbr = fr[..., None, :] * bt_re - fi[..., None, :] * bt_im
    bbi = fr[..., None, :] * bt_im + fi[..., None, :] * bt_re
    ct_re, ct_im = jnp.swapaxes(c_re, -1, -2), jnp.swapaxes(c_im, -1, -2)
    slab = lambda x: x.reshape((DEPTH, 2, N_SLAB, G_SLAB) + x.shape[3:])

    rows = []
    for s in range(CH_T):
        pr, pi = pick(CH_T - 1 - s, s)
        pr, pi = pr[..., None, :], pi[..., None, :]
        rows.append(slab(jnp.concatenate([pr * bbr - pi * bbi, pr * bbi + pi * bbr], axis=-1)))
    wc = jnp.stack(rows, axis=3).reshape(DEPTH, 2, N_SLAB, CAT, LANES)

    def c_times_powers(n_fwd, n_bwd):
        re, im = [], []
        for j in range(CH_T):
            pr, pi = pick(n_fwd[j], n_bwd[j])
            pr, pi = pr[..., None], pi[..., None]
            re.append(ct_re * pr - ct_im * pi)
            im.append(-(ct_re * pi + ct_im * pr))
        parts = [slab(jnp.concatenate(x, axis=-1)) for x in (re, im)]
        return jnp.stack(parts, axis=3).reshape(DEPTH, 2, N_SLAB, CAT, LANES)

    steps = list(range(CH_T))
    vk = c_times_powers(steps, steps[::-1])
    vc = c_times_powers([t + 1 for t in steps], [CH_T - t for t in steps])

    def lanes(n):
        r, i = apow(n)
        return jnp.stack([r, i], axis=2).reshape(DEPTH, 2, 2, N_SLAB, 1, SLAB_STATE)

    r8 = np.arange(SUBLANES)
    steps = []
    for j in range(3):
        sh = 1 << j
        has_partner = np.stack([r8 >= sh, r8 < SUBLANES - sh]).astype(np.float32)
        steps.append(lanes(CH_T * sh) * has_partner[None, :, None, None, :, None])
    ast = jnp.stack(steps, axis=2)
    ast = ast.transpose(0, 1, 4, 2, 3, 5, 6)
    per_row = jnp.concatenate([lanes(CH_T * (r + 1)) for r in range(SUBLANES)], axis=4)
    pc = jnp.stack([per_row[:, 0], per_row[:, 1, :, :, ::-1]], axis=1)
    pc = pc.transpose(0, 1, 3, 2, 4, 5)
    return wc, vk, vc, ast, pc


def _gelu_tanh(x):
    c = math.sqrt(2.0 / math.pi)
    return 0.5 * x * (1.0 + jnp.tanh(c * (x + 0.044715 * (x * x * x))))


def _merge_kernel(h_ref, yf_ref, yb_ref, us_ref, at_ref, gs_ref, ga_ref, d_ref,
                  wglu_ref, wbs_ref, wba_ref, wout_ref, o_ref):
    lanes = lambda ref: jnp.concatenate([ref[s] for s in range(N_SLAB)], axis=1)
    y = lanes(yf_ref) + lanes(yb_ref) + d_ref[...] * lanes(us_ref)
    z = _gelu_tanh(y)
    gl = jnp.dot(z.astype(bf16), wglu_ref[...], preferred_element_type=f32)
    ys = (z * jax.nn.sigmoid(gl)).astype(bf16)
    m1 = jnp.dot(ys, wbs_ref[...], preferred_element_type=f32)
    m2 = jnp.dot(at_ref[...], wba_ref[...], preferred_element_type=f32)
    merged = (jax.nn.sigmoid(gs_ref[...].astype(f32)) * m1
              + jax.nn.sigmoid(ga_ref[...].astype(f32)) * m2)
    out = h_ref[...] + jnp.dot(merged.astype(bf16), wout_ref[...], preferred_element_type=f32)
    rows = pl.program_id(0) * TM + lax.broadcasted_iota(jnp.int32, (TM, 1), 0)
    o_ref[...] = jnp.where(rows >= PAD, out, 0.0)


def _merge(l, h, yf, yb, us, proj, attn, d, wglu, wbs, wba, wout):
    row = lambda w, col: pl.BlockSpec((TM, w), lambda i: (i, col))
    slab = pl.BlockSpec((N_SLAB, TM, LANES), lambda i: (0, i, 0))
    full = lambda r, c: _resident((None, r, c), lambda i: (l, 0, 0))
    return pl.pallas_call(
        _merge_kernel,
        grid=(LP // TM,),
        in_specs=[
            row(D_MODEL, 0), slab, slab, slab, row(ATTN_Q, 0),
            row(D_MODEL, COL_GS // D_MODEL), row(D_MODEL, COL_GA // D_MODEL),
            full(1, SSM_WIDTH), full(SSM_WIDTH, SSM_WIDTH), full(SSM_WIDTH, D_MODEL),
            full(ATTN_Q, D_MODEL), full(D_MODEL, D_MODEL),
        ],
        out_specs=row(D_MODEL, 0),
        out_shape=jax.ShapeDtypeStruct((LP, D_MODEL), f32),
        compiler_params=_cparams(("parallel",)),
        name="merge",
    )(h, yf, yb, us, attn, proj, proj, d, wglu, wbs, wba, wout)


def kernel(x, meta_tokens, ffn1_norm, ffn1_w_gate, ffn1_w_up, ffn1_w_down, mix_norm, w_in, ssm_lam_re, ssm_lam_im, ssm_log_dt, ssm_b_re, ssm_b_im, ssm_c_re, ssm_c_im, ssm_d, ssm_w_glu, attn_sink, w_branch_ssm, w_branch_attn, w_out, ffn2_norm, ffn2_w_gate, ffn2_w_up, ffn2_w_down, final_norm):
    assert x.shape == (1, SEQ, D_MODEL)
    h = jnp.concatenate([jnp.zeros((PAD, D_MODEL), f32), meta_tokens.astype(f32), x[0]], axis=0)

    col_scale = jnp.where(jnp.arange(IN_COLS) < ATTN_Q, LOG2E * HEAD_DIM ** -0.5, 1.0).astype(f32)
    w_in_b = (w_in * col_scale).astype(bf16)
    cast = lambda w: w.astype(bf16)
    f1g, f1u, f1d = cast(ffn1_w_gate), cast(ffn1_w_up), cast(ffn1_w_down)
    f2g, f2u, f2d = cast(ffn2_w_gate), cast(ffn2_w_up), cast(ffn2_w_down)
    wglu, wbs, wba, wo = cast(ssm_w_glu), cast(w_branch_ssm), cast(w_branch_attn), cast(w_out)
    ssm_mats = _ssm_params(ssm_lam_re, ssm_lam_im, ssm_log_dt, ssm_b_re, ssm_b_im, ssm_c_re, ssm_c_im)
    bias = _attn_bias()

    n1, nm, n2, dskip = (v[:, None, :] for v in (ffn1_norm, mix_norm, ffn2_norm, ssm_d))
    for l in range(DEPTH):
        h = _ffn(l, h, n1, f1g, f1u, f1d)
        proj, us = _inproj(l, h, nm, w_in_b)
        attn = _attn(proj, attn_sink[l], bias)
        yf = _ssm(l, 0, us, *ssm_mats)
        yb = _ssm(l, 1, us, *ssm_mats)
        h = _merge(l, h, yf, yb, us, proj, attn, dskip, wglu, wbs, wba, wo)
        if l + 1 < DEPTH:
            h = _ffn(l, h, n2, f2g, f2u, f2d)
    return _ffn_final(DEPTH - 1, h, n2, f2g, f2u, f2d, final_norm[None])[None]
```

```python
import functools
import math

import numpy as np
import jax
import jax.numpy as jnp
from jax import lax
from jax.experimental import pallas as pl
from jax.experimental.pallas import tpu as pltpu

f32 = jnp.float32
bf16 = jnp.bfloat16

D_MODEL = 1024
SEQ = 16384
DEPTH = 4
N_META = 16
N_HEADS = 16
N_KV_HEADS = 4
HEAD_DIM = 64
Q_GROUP = N_HEADS // N_KV_HEADS
WINDOW = 128
BLOCK = 128
SSM_WIDTH = 512
SSM_GROUP_CH = 16
SSM_GROUPS = 32
SSM_STATE = 64
D_FF = 2816
EPS = 1e-6
NEG = -1e30
LOG2E = math.log2(math.e)

ATTN_Q = N_HEADS * HEAD_DIM
ATTN_KV = N_KV_HEADS * HEAD_DIM
IN_COLS = 4096
COL_K = ATTN_Q
COL_V = ATTN_Q + ATTN_KV
COL_U = ATTN_Q + 2 * ATTN_KV
COL_GS = COL_U + SSM_WIDTH
COL_GA = COL_GS + D_MODEL

PAD = BLOCK - N_META
LP = PAD + N_META + SEQ
NB = LP // BLOCK

LANES = 128
SUBLANES = 8
VMEM_LIMIT = 56 * 1024 * 1024

TM = 688
TM_FFN = TM
MXU_TILE = 256
FF_CHUNK = 2 * MXU_TILE
TN_IN = 4 * MXU_TILE
T_FINAL = 512
HEADS_PER_DOT = 2
QB_STEP = 3

CH_T = SUBLANES
N_SLAB = SSM_WIDTH // LANES
G_SLAB = SSM_GROUPS // N_SLAB
SLAB_STATE = G_SLAB * SSM_STATE
CAT = CH_T * LANES
T_SSM = LP // 3
NT_SSM = LP // T_SSM
R_SSM = T_SSM // CH_T
R_VREG = R_SSM // SUBLANES
EXPAND_ROWS = 256


def _cparams(sem, fuse_inputs=None):
    return pltpu.CompilerParams(dimension_semantics=sem, vmem_limit_bytes=VMEM_LIMIT,
                                allow_input_fusion=fuse_inputs)


def _rmsnorm(x, g):
    ms = jnp.mean(x * x, axis=-1, keepdims=True)
    return x * lax.rsqrt(ms + EPS) * g


def _ffn_kernel(x_ref, g_ref, wg_ref, wu_ref, wd_ref, *rest):
    o_ref = rest[-1]
    x = x_ref[...]
    xn = _rmsnorm(x, g_ref[...]).astype(bf16)
    acc = None
    for c0 in range(0, D_FF, FF_CHUNK):
        c1 = min(c0 + FF_CHUNK, D_FF)
        a = jnp.dot(xn, wg_ref[:, c0:c1], preferred_element_type=f32)
        b = jnp.dot(xn, wu_ref[:, c0:c1], preferred_element_type=f32)
        mid = (a * jax.nn.sigmoid(a) * b).astype(bf16)
        part = jnp.dot(mid, wd_ref[c0:c1, :], preferred_element_type=f32)
        acc = part if acc is None else acc + part
    out = x + 0.5 * acc
    o_ref[...] = _rmsnorm(out, rest[0][...]) if len(rest) == 2 else out


def _resident(shape, index_map):
    return pl.BlockSpec(shape, index_map, pipeline_mode=pl.Buffered(1))


def _ffn(l, h, g, wg, wu, wd):
    return pl.pallas_call(
        _ffn_kernel,
        grid=(LP // TM_FFN,),
        in_specs=[
            pl.BlockSpec((TM_FFN, D_MODEL), lambda i: (i, 0)),
            _resident((None, 1, D_MODEL), lambda i: (l, 0, 0)),
            _resident((None, D_MODEL, D_FF), lambda i: (l, 0, 0)),
            _resident((None, D_MODEL, D_FF), lambda i: (l, 0, 0)),
            _resident((None, D_FF, D_MODEL), lambda i: (l, 0, 0)),
        ],
        out_specs=pl.BlockSpec((TM_FFN, D_MODEL), lambda i: (i, 0)),
        out_shape=jax.ShapeDtypeStruct((LP, D_MODEL), f32),
        compiler_params=_cparams(("parallel",), [False, False, True, True, True]),
        name="ffn",
    )(h, g, wg, wu, wd)


def _ffn_final(l, h, g, wg, wu, wd, g_final):
    weights = lambda r, c: _resident((None, r, c), lambda i: (l, 0, 0))
    return pl.pallas_call(
        _ffn_kernel,
        grid=(SEQ // T_FINAL,),
        in_specs=[
            pl.BlockSpec((pl.Element(T_FINAL), pl.Element(D_MODEL)),
                         lambda i: (pl.multiple_of(BLOCK + i * T_FINAL, BLOCK), 0)),
            weights(1, D_MODEL), weights(D_MODEL, D_FF), weights(D_MODEL, D_FF), weights(D_FF, D_MODEL),
            _resident((1, D_MODEL), lambda i: (0, 0)),
        ],
        out_specs=pl.BlockSpec((T_FINAL, D_MODEL), lambda i: (i, 0)),
        out_shape=jax.ShapeDtypeStruct((SEQ, D_MODEL), f32),
        compiler_params=_cparams(("parallel",)),
        name="ffn_final",
    )(h, g, wg, wu, wd, g_final)


def _inproj_kernel(x_ref, g_ref, w_ref, o_ref, us_ref):
    xn = _rmsnorm(x_ref[...], g_ref[...]).astype(bf16)
    for c0 in range(0, IN_COLS, TN_IN):
        res = jnp.dot(xn, w_ref[:, c0:c0 + TN_IN], preferred_element_type=f32)
        o_ref[:, c0:c0 + TN_IN] = res.astype(bf16)
        for s in range(N_SLAB):
            u0 = COL_U + s * LANES
            if c0 <= u0 < c0 + TN_IN:
                us_ref[s] = res[:, u0 - c0:u0 - c0 + LANES]


def _inproj(l, h, g, w):
    return pl.pallas_call(
        _inproj_kernel,
        grid=(LP // TM,),
        in_specs=[
            pl.BlockSpec((TM, D_MODEL), lambda i: (i, 0)),
            _resident((None, 1, D_MODEL), lambda i: (l, 0, 0)),
            _resident((None, D_MODEL, IN_COLS), lambda i: (l, 0, 0)),
        ],
        out_specs=[pl.BlockSpec((TM, IN_COLS), lambda i: (i, 0)),
                   pl.BlockSpec((N_SLAB, TM, LANES), lambda i: (0, i, 0))],
        out_shape=[jax.ShapeDtypeStruct((LP, IN_COLS), bf16),
                   jax.ShapeDtypeStruct((N_SLAB, LP, LANES), f32)],
        compiler_params=_cparams(("parallel",)),
        name="inproj",
    )(h, g, w)


def _attn_block(sink_ref, q_ref, k_refs, v_refs, bias_ref, o_ref):
    k_all, v_all = [], []
    for kh in range(N_KV_HEADS):
        sl = slice(kh * HEAD_DIM, (kh + 1) * HEAD_DIM)
        k_all.append(jnp.concatenate([r[:, sl] for r in k_refs], axis=0))
        v_all.append(jnp.concatenate([r[:, sl] for r in v_refs], axis=0))

    def logits(t):
        heads = range(t * HEADS_PER_DOT, (t + 1) * HEADS_PER_DOT)
        q = jnp.concatenate([q_ref[:, h * HEAD_DIM:(h + 1) * HEAD_DIM] for h in heads], axis=0)
        return lax.dot_general(q, k_all[heads[0] // Q_GROUP], (((1,), (1,)), ((), ())),
                               preferred_element_type=f32)

    sink_lane = lax.broadcasted_iota(jnp.int32, (BLOCK, BLOCK), 1) == 0

    def softmax(t, s):
        ps, denoms = [], []
        for j in range(HEADS_PER_DOT):
            h = t * HEADS_PER_DOT + j
            sh = s[j * BLOCK:(j + 1) * BLOCK] + bias_ref[0, h]
            meta = jnp.where(sink_lane, sink_ref[h] * LOG2E, sh[:, 3 * BLOCK:])
            sh = jnp.concatenate([sh[:, :3 * BLOCK], meta], axis=1)
            p = jnp.exp2(sh - jnp.max(sh, axis=-1, keepdims=True))
            denoms.append(jnp.sum(p, axis=-1, keepdims=True))
            ps.append(p.astype(bf16))
        return jnp.concatenate(ps, axis=0), denoms

    def finish(t, p, denoms):
        o = jnp.dot(p, v_all[t * HEADS_PER_DOT // Q_GROUP], preferred_element_type=f32)
        outs = [o[j * BLOCK:(j + 1) * BLOCK] / denoms[j] for j in range(HEADS_PER_DOT)]
        w = HEADS_PER_DOT * HEAD_DIM
        o_ref[:, t * w:(t + 1) * w] = jnp.concatenate(outs, axis=1).astype(bf16)

    return logits, softmax, finish


def _attn_kernel(sink_ref, q_ref, *refs):
    n_kv = QB_STEP + 2
    k_refs, km_ref = refs[:n_kv], refs[n_kv]
    v_refs, vm_ref = refs[n_kv + 1:2 * n_kv + 1], refs[2 * n_kv + 1]
    bias_refs = refs[2 * n_kv + 2:2 * n_kv + 2 + QB_STEP]
    o_ref = refs[-1]
    blocks = []
    for i in range(QB_STEP):
        rows = slice(i * BLOCK, (i + 1) * BLOCK)
        blocks.append(_attn_block(sink_ref, q_ref.at[rows], list(k_refs[i:i + 3]) + [km_ref],
                                  list(v_refs[i:i + 3]) + [vm_ref], bias_refs[i], o_ref.at[rows]))

    per_block = N_HEADS // HEADS_PER_DOT
    work = [(blk, t) for blk in blocks for t in range(per_block)]
    logits = lambda n: work[n][0][0](work[n][1])
    softmax = lambda n, s: work[n][0][1](work[n][1], s)
    s_q = {0: logits(0), 1: logits(1)}
    p_q = {0: softmax(0, s_q.pop(0))}
    for n in range(len(work)):
        if n + 2 < len(work):
            s_q[n + 2] = logits(n + 2)
        if n + 1 < len(work):
            p_q[n + 1] = softmax(n + 1, s_q.pop(n + 1))
        work[n][0][2](work[n][1], *p_q.pop(n))


def _attn(proj, sink, bias):
    n_steps = NB // QB_STEP
    clamp = lambda b: jnp.clip(b, 0, NB - 1)
    kv = lambda col, off: pl.BlockSpec((BLOCK, ATTN_KV), lambda n: (clamp(n * QB_STEP + off), col))
    first = lambda col: pl.BlockSpec((BLOCK, ATTN_KV), lambda n: (0, col))
    ck, cv = COL_K // ATTN_KV, COL_V // ATTN_KV
    offs = range(-1, QB_STEP + 1)

    def bias_spec(i):
        def variant(n):
            b = n * QB_STEP + i
            return jnp.where(b == 0, 0, jnp.where(b == 1, 1, jnp.where(b == NB - 1, 3, 2)))
        return pl.BlockSpec((1, N_HEADS, BLOCK, 4 * BLOCK), lambda n: (variant(n), 0, 0, 0),
                            pipeline_mode=pl.Buffered(1))

    return pl.pallas_call(
        _attn_kernel,
        grid=(n_steps,),
        in_specs=[pl.BlockSpec(memory_space=pltpu.SMEM),
                  pl.BlockSpec((QB_STEP * BLOCK, ATTN_Q), lambda n: (n, 0))]
                 + [kv(ck, o) for o in offs] + [first(ck)]
                 + [kv(cv, o) for o in offs] + [first(cv)]
                 + [bias_spec(i) for i in range(QB_STEP)],
        out_specs=pl.BlockSpec((QB_STEP * BLOCK, ATTN_Q), lambda n: (n, 0)),
        out_shape=jax.ShapeDtypeStruct((LP, ATTN_Q), bf16),
        compiler_params=_cparams(("parallel",)),
        name="attn",
    )(sink, proj, *([proj] * (2 * (QB_STEP + 3))), *([bias] * QB_STEP))


def _attn_bias():
    qi = np.arange(BLOCK)[:, None]
    sj = np.arange(3 * BLOCK)[None, :]
    dist = np.abs(qi + BLOCK - sj)
    slopes = 2.0 ** (-8.0 * np.arange(1, N_HEADS + 1) / N_HEADS)
    band = -LOG2E * slopes[:, None, None] * dist[None].astype(np.float64)
    in_win = (dist <= WINDOW)[None]
    meta = np.where(np.arange(BLOCK) >= PAD, 0.0, NEG)[None, None, :]
    meta = np.broadcast_to(meta, (N_HEADS, BLOCK, BLOCK))
    out = []
    for blk_ok in ((False, False, True), (False, True, True), (True, True, True), (True, True, False)):
        kvalid = np.repeat(np.asarray(blk_ok), BLOCK)[None, None, :]
        b = np.where(in_win & kvalid, band, NEG)
        out.append(np.concatenate([b, meta], axis=-1))
    return jnp.asarray(np.stack(out), dtype=f32)


def _cmul(ar, ai, br, bi):
    return ar * br - ai * bi, ar * bi + ai * br


def _expand_block_diag(src_ref, dst_ref, row_inner, col_inner):
    lane = lax.broadcasted_iota(jnp.int32, (LANES, CAT), 0)
    col = lax.broadcasted_iota(jnp.int32, (LANES, CAT), 1)
    src_lane = (col // (G_SLAB * col_inner)) * col_inner + col % col_inner
    spread = jnp.where(src_lane == lane, 1.0, 0.0).astype(bf16)
    col_g = (lax.broadcasted_iota(jnp.int32, (1, CAT), 1) // col_inner) % G_SLAB
    for r0 in range(0, CAT, EXPAND_ROWS):
        row_g = ((r0 + lax.broadcasted_iota(jnp.int32, (EXPAND_ROWS, 1), 0)) // row_inner) % G_SLAB
        full = jnp.dot(src_ref[r0:r0 + EXPAND_ROWS, :].astype(bf16), spread, preferred_element_type=f32)
        dst_ref[r0:r0 + EXPAND_ROWS, :] = jnp.where(row_g == col_g, full, 0.0).astype(bf16)


def _ssm_kernel(reverse, u_ref, wc_ref, vk_ref, vc_ref, ast_ref, pc_ref, y_ref,
                w_ref, m_ref, v_ref, ucat_ref, loc_ref, sp_ref, ycat_ref, carry_ref):
    @pl.when(pl.program_id(1) == 0)
    def _():
        carry_ref[...] = jnp.zeros_like(carry_ref)
        _expand_block_diag(wc_ref, w_ref, SSM_GROUP_CH, SSM_STATE)
        _expand_block_diag(vk_ref, v_ref, SSM_STATE, SSM_GROUP_CH)
        s0 = (0 if reverse else CH_T - 1) * LANES
        strip = jnp.dot(w_ref[s0:s0 + LANES, :], v_ref[...], preferred_element_type=f32).astype(bf16)
        for s in range(CH_T):
            lo, hi = (0, (s + 1) * LANES) if reverse else (s * LANES, CAT)
            src = (CH_T - 1 - s) * LANES if reverse else 0
            if lo > 0:
                m_ref[s * LANES:(s + 1) * LANES, :lo] = jnp.zeros((LANES, lo), bf16)
            m_ref[s * LANES:(s + 1) * LANES, lo:hi] = strip[:, src:src + hi - lo]
            if hi < CAT:
                m_ref[s * LANES:(s + 1) * LANES, hi:] = jnp.zeros((LANES, CAT - hi), bf16)
        _expand_block_diag(vc_ref, v_ref, SSM_STATE, SSM_GROUP_CH)

    def rows8(k):
        return pl.ds(pl.multiple_of(k * SUBLANES, SUBLANES), SUBLANES)

    def token_rows(k, t):
        return pl.ds(k * (SUBLANES * CH_T) + t, SUBLANES, stride=CH_T)

    def gather(k, _):
        for t in range(CH_T):
            ucat_ref[rows8(k), t * LANES:(t + 1) * LANES] = u_ref[token_rows(k, t), :]
        return 0

    lax.fori_loop(0, R_VREG, gather, 0, unroll=True)
    ucat = ucat_ref[...].astype(bf16)
    loc_ref[...] = jnp.dot(ucat, w_ref[...], preferred_element_type=f32)
    for c0 in range(0, CAT, MXU_TILE):
        k = slice(c0, CAT) if reverse else slice(0, c0 + MXU_TILE)
        ycat_ref[:, c0:c0 + MXU_TILE] = jnp.dot(ucat[:, k], m_ref[k, c0:c0 + MXU_TILE],
                                                preferred_element_type=f32)

    row = lax.broadcasted_iota(jnp.int32, (SUBLANES, SLAB_STATE), 0)
    edge = SUBLANES - 1 if reverse else 0
    last = 0 if reverse else SUBLANES - 1
    toward = lambda x, n: pltpu.roll(x, (SUBLANES - n) if reverse else n, 0)

    def scan(n, carry):
        k = R_VREG - 1 - n if reverse else n
        xr = loc_ref[rows8(k), :SLAB_STATE]
        xi = loc_ref[rows8(k), SLAB_STATE:]
        for j in range(3):
            tr, ti = _cmul(ast_ref[j, 0], ast_ref[j, 1], toward(xr, 1 << j), toward(xi, 1 << j))
            xr, xi = xr + tr, xi + ti
        cr, ci = carry
        tr, ti = _cmul(pc_ref[0], pc_ref[1], cr, ci)
        xr, xi = xr + tr, xi + ti
        sp_ref[rows8(k), :SLAB_STATE] = jnp.where(row == edge, cr, toward(xr, 1))
        sp_ref[rows8(k), SLAB_STATE:] = jnp.where(row == edge, ci, toward(xi, 1))
        bcast = lambda x: jnp.broadcast_to(x[last:last + 1], (SUBLANES, SLAB_STATE))
        return bcast(xr), bcast(xi)

    cr, ci = lax.fori_loop(0, R_VREG, scan, (carry_ref[0], carry_ref[1]), unroll=True)
    carry_ref[0] = cr
    carry_ref[1] = ci
    ycat_ref[...] += jnp.dot(sp_ref[...].astype(bf16), v_ref[...], preferred_element_type=f32)

    def scatter(k, _):
        for t in range(CH_T):
            y_ref[token_rows(k, t), :] = ycat_ref[rows8(k), t * LANES:(t + 1) * LANES]
        return 0

    lax.fori_loop(0, R_VREG, scatter, 0, unroll=True)


def _ssm(l, d, us, wc, vk, vc, ast, pc):
    tile = (lambda i: NT_SSM - 1 - i) if d else (lambda i: i)
    rows = pl.BlockSpec((None, T_SSM, LANES), lambda s, i: (s, tile(i), 0))
    mat = pl.BlockSpec((None, None, None, CAT, LANES), lambda s, i: (l, d, s, 0, 0))
    return pl.pallas_call(
        functools.partial(_ssm_kernel, bool(d)),
        grid=(N_SLAB, NT_SSM),
        in_specs=[
            rows, mat, mat, mat,
            pl.BlockSpec((None, None, None, 3, 2, SUBLANES, SLAB_STATE), lambda s, i: (l, d, s, 0, 0, 0, 0)),
            pl.BlockSpec((None, None, None, 2, SUBLANES, SLAB_STATE), lambda s, i: (l, d, s, 0, 0, 0)),
        ],
        out_specs=rows,
        out_shape=jax.ShapeDtypeStruct((N_SLAB, LP, LANES), f32),
        scratch_shapes=[
            pltpu.VMEM((CAT, CAT), bf16),
            pltpu.VMEM((CAT, CAT), bf16),
            pltpu.VMEM((CAT, CAT), bf16),
            pltpu.VMEM((R_SSM, CAT), f32),
            pltpu.VMEM((R_SSM, CAT), f32),
            pltpu.VMEM((R_SSM, CAT), f32),
            pltpu.VMEM((R_SSM, CAT), f32),
            pltpu.VMEM((2, SUBLANES, SLAB_STATE), f32),
        ],
        compiler_params=_cparams(("parallel", "arbitrary")),
        name="ssm",
    )(us, wc, vk, vc, ast, pc)


def _ssm_params(lam_re, lam_im, log_dt, b_re, b_im, c_re, c_im):
    dt = jnp.exp(log_dt)[..., None]

    def apow(n):
        mag = jnp.exp(lam_re * dt * n)
        return mag * jnp.cos(lam_im * dt * n), mag * jnp.sin(lam_im * dt * n)

    pows = [apow(n) for n in range(CH_T + 1)]
    forward = (jnp.arange(2) == 0)[None, :, None, None]

    def pick(n_fwd, n_bwd):
        return (jnp.where(forward, pows[n_fwd][0], pows[n_bwd][0]),
                jnp.where(forward, pows[n_fwd][1], pows[n_bwd][1]))

    ar, ai = pows[1]
    den = lam_re * lam_re + lam_im * lam_im
    fr = ((ar - 1.0) * lam_re + ai * lam_im) / den
    fi = (ai * lam_re - (ar - 1.0) * lam_im) / den
    bt_re, bt_im = jnp.swapaxes(b_re, -1, -2), jnp.swapaxes(b_im, -1, -2)
    bbr = fr[..., None, :] * bt_re - fi[..., None, :] * bt_im
    bbi = fr[..., None, :] * bt_im + fi[..., None, :] * bt_re
    ct_re, ct_im = jnp.swapaxes(c_re, -1, -2), jnp.swapaxes(c_im, -1, -2)
    slab = lambda x: x.reshape((DEPTH, 2, N_SLAB, G_SLAB) + x.shape[3:])

    rows = []
    for s in range(CH_T):
        pr, pi = pick(CH_T - 1 - s, s)
        pr, pi = pr[..., None, :], pi[..., None, :]
        rows.append(slab(jnp.concatenate([pr * bbr - pi * bbi, pr * bbi + pi * bbr], axis=-1)))
    wc = jnp.stack(rows, axis=3).reshape(DEPTH, 2, N_SLAB, CAT, LANES)

    def c_times_powers(n_fwd, n_bwd):
        re, im = [], []
        for j in range(CH_T):
            pr, pi = pick(n_fwd[j], n_bwd[j])
            pr, pi = pr[..., None], pi[..., None]
            re.append(ct_re * pr - ct_im * pi)
            im.append(-(ct_re * pi + ct_im * pr))
        parts = [slab(jnp.concatenate(x, axis=-1)) for x in (re, im)]
        return jnp.stack(parts, axis=3).reshape(DEPTH, 2, N_SLAB, CAT, LANES)

    steps = list(range(CH_T))
    vk = c_times_powers(steps, steps[::-1])
    vc = c_times_powers([t + 1 for t in steps], [CH_T - t for t in steps])

    def lanes(n):
        r, i = apow(n)
        return jnp.stack([r, i], axis=2).reshape(DEPTH, 2, 2, N_SLAB, 1, SLAB_STATE)

    r8 = np.arange(SUBLANES)
    steps = []
    for j in range(3):
        sh = 1 << j
        has_partner = np.stack([r8 >= sh, r8 < SUBLANES - sh]).astype(np.float32)
        steps.append(lanes(CH_T * sh) * has_partner[None, :, None, None, :, None])
    ast = jnp.stack(steps, axis=2)
    ast = ast.transpose(0, 1, 4, 2, 3, 5, 6)
    per_row = jnp.concatenate([lanes(CH_T * (r + 1)) for r in range(SUBLANES)], axis=4)
    pc = jnp.stack([per_row[:, 0], per_row[:, 1, :, :, ::-1]], axis=1)
    pc = pc.transpose(0, 1, 3, 2, 4, 5)
    return wc, vk, vc, ast, pc


def _gelu_tanh(x):
    c = math.sqrt(2.0 / math.pi)
    return 0.5 * x * (1.0 + jnp.tanh(c * (x + 0.044715 * (x * x * x))))


def _merge_kernel(h_ref, yf_ref, yb_ref, us_ref, at_ref, gs_ref, ga_ref, d_ref,
                  wglu_ref, wbs_ref, wba_ref, wout_ref, o_ref):
    lanes = lambda ref: jnp.concatenate([ref[s] for s in range(N_SLAB)], axis=1)
    y = lanes(yf_ref) + lanes(yb_ref) + d_ref[...] * lanes(us_ref)
    z = _gelu_tanh(y)
    gl = jnp.dot(z.astype(bf16), wglu_ref[...], preferred_element_type=f32)
    ys = (z * jax.nn.sigmoid(gl)).astype(bf16)
    m1 = jnp.dot(ys, wbs_ref[...], preferred_element_type=f32)
    m2 = jnp.dot(at_ref[...], wba_ref[...], preferred_element_type=f32)
    merged = (jax.nn.sigmoid(gs_ref[...].astype(f32)) * m1
              + jax.nn.sigmoid(ga_ref[...].astype(f32)) * m2)
    out = h_ref[...] + jnp.dot(merged.astype(bf16), wout_ref[...], preferred_element_type=f32)
    rows = pl.program_id(0) * TM + lax.broadcasted_iota(jnp.int32, (TM, 1), 0)
    o_ref[...] = jnp.where(rows >= PAD, out, 0.0)


def _merge(l, h, yf, yb, us, proj, attn, d, wglu, wbs, wba, wout):
    row = lambda w, col: pl.BlockSpec((TM, w), lambda i: (i, col))
    slab = pl.BlockSpec((N_SLAB, TM, LANES), lambda i: (0, i, 0))
    full = lambda r, c: _resident((None, r, c), lambda i: (l, 0, 0))
    return pl.pallas_call(
        _merge_kernel,
        grid=(LP // TM,),
        in_specs=[
            row(D_MODEL, 0), slab, slab, slab, row(ATTN_Q, 0),
            row(D_MODEL, COL_GS // D_MODEL), row(D_MODEL, COL_GA // D_MODEL),
            full(1, SSM_WIDTH), full(SSM_WIDTH, SSM_WIDTH), full(SSM_WIDTH, D_MODEL),
            full(ATTN_Q, D_MODEL), full(D_MODEL, D_MODEL),
        ],
        out_specs=row(D_MODEL, 0),
        out_shape=jax.ShapeDtypeStruct((LP, D_MODEL), f32),
        compiler_params=_cparams(("parallel",)),
        name="merge",
    )(h, yf, yb, us, attn, proj, proj, d, wglu, wbs, wba, wout)


def kernel(x, meta_tokens, ffn1_norm, ffn1_w_gate, ffn1_w_up, ffn1_w_down, mix_norm, w_in, ssm_lam_re, ssm_lam_im, ssm_log_dt, ssm_b_re, ssm_b_im, ssm_c_re, ssm_c_im, ssm_d, ssm_w_glu, attn_sink, w_branch_ssm, w_branch_attn, w_out, ffn2_norm, ffn2_w_gate, ffn2_w_up, ffn2_w_down, final_norm):
    assert x.shape == (1, SEQ, D_MODEL)
    h = jnp.concatenate([jnp.zeros((PAD, D_MODEL), f32), meta_tokens.astype(f32), x[0]], axis=0)

    col_scale = jnp.where(jnp.arange(IN_COLS) < ATTN_Q, LOG2E * HEAD_DIM ** -0.5, 1.0).astype(f32)
    w_in_b = (w_in * col_scale).astype(bf16)
    cast = lambda w: w.astype(bf16)
    f1g, f1u, f1d = cast(ffn1_w_gate), cast(ffn1_w_up), cast(ffn1_w_down)
    f2g, f2u, f2d = cast(ffn2_w_gate), cast(ffn2_w_up), cast(ffn2_w_down)
    wglu, wbs, wba, wo = cast(ssm_w_glu), cast(w_branch_ssm), cast(w_branch_attn), cast(w_out)
    ssm_mats = _ssm_params(ssm_lam_re, ssm_lam_im, ssm_log_dt, ssm_b_re, ssm_b_im, ssm_c_re, ssm_c_im)
    bias = _attn_bias()

    n1, nm, n2, dskip = (v[:, None, :] for v in (ffn1_norm, mix_norm, ffn2_norm, ssm_d))
    for l in range(DEPTH):
        h = _ffn(l, h, n1, f1g, f1u, f1d)
        proj, us = _inproj(l, h, nm, w_in_b)
        attn = _attn(proj, attn_sink[l], bias)
        yf = _ssm(l, 0, us, *ssm_mats)
        yb = _ssm(l, 1, us, *ssm_mats)
        h = _merge(l, h, yf, yb, us, proj, attn, dskip, wglu, wbs, wba, wo)
        if l + 1 < DEPTH:
            h = _ffn(l, h, n2, f2g, f2u, f2d)
    return _ffn_final(DEPTH - 1, h, n2, f2g, f2u, f2d, final_norm[None])[None]
```
